```python
import jax
import jax.numpy as jnp
from jax import lax
import numpy as np

D_MODEL = 2048
BATCH = 4
SEQ = 4096
DEPTH = 4

D_HGRN = D_MODEL // 2
HGRN_HEAD_DIM = 128
HGRN_HEADS = D_HGRN // HGRN_HEAD_DIM
D_POOL = D_MODEL - D_HGRN
POOL_WINDOWS = (2, 4, 8, 16)
POOL_GROUPS = len(POOL_WINDOWS)
POOL_GROUP_DIM = D_POOL // POOL_GROUPS
D_MIX = D_HGRN + D_POOL
D_IN = 4 * D_HGRN + D_POOL
D_FF = ((8 * D_MODEL // 3 + 255) // 256) * 256
CONV_WIDTH = 3
CHUNK = 64
N_MOD = 6
EPS = 1e-6

kernel_name = 'hymba_style_hgrn2_pool_hybrid'


def rms_norm(x, gain):
    xf = x.astype(jnp.float32)
    y = xf * lax.rsqrt(jnp.mean(xf * xf, axis=-1, keepdims=True) + EPS)
    return (y * gain.astype(jnp.float32)).astype(x.dtype)


def hgrn2_chunked(q, k, v, log_f):
    B, S, H, Dk = q.shape
    Dv = v.shape[-1]
    n = S // CHUNK

    def to_chunks(t):
        return t.reshape(B, n, CHUNK, H, t.shape[-1]).transpose(1, 0, 3, 2, 4)

    qc, kc, vc, gc = (to_chunks(t) for t in (q, k, v, log_f))
    causal = jnp.tril(jnp.ones((CHUNK, CHUNK), dtype=bool))[:, :, None]

    def step(state, inp):
        qn, kn, vn, gn = inp
        b = jnp.cumsum(gn, axis=-2)
        diff = b[:, :, :, None, :] - b[:, :, None, :, :]
        decay = jnp.where(causal, jnp.exp(jnp.where(causal, diff, 0.0)), 0.0)
        scores = jnp.einsum('bhtd,bhsd,bhtsd->bhts', qn, kn, decay)
        o = (jnp.einsum('bhts,bhsv->bhtv', scores, vn)
             + jnp.einsum('bhtd,bhdv->bhtv', qn * jnp.exp(b), state))
        b_last = b[:, :, -1:, :]
        state = (jnp.exp(b_last[:, :, 0, :, None]) * state
                 + jnp.einsum('bhsd,bhsv->bhdv', kn * jnp.exp(b_last - b), vn))
        return state, o

    s0 = jnp.zeros((B, H, Dk, Dv), jnp.float32)
    _, o = lax.scan(step, s0, (qc, kc, vc, gc))
    return o.transpose(1, 0, 3, 2, 4).reshape(B, S, H, Dv)


def hgrn2_branch(q, f, i, g, lb, norm_g):
    B, S, _ = q.shape

    def heads(t):
        return t.astype(jnp.float32).reshape(B, S, HGRN_HEADS, HGRN_HEAD_DIM)

    qh = heads(jax.nn.silu(q)) * (HGRN_HEAD_DIM ** -0.5)
    z = heads(f)
    lbh = lb.astype(jnp.float32).reshape(HGRN_HEADS, HGRN_HEAD_DIM)
    log_f = jax.nn.log_sigmoid(z) + jnp.log1p(lbh * jnp.exp(-z))
    kh = (1.0 - lbh) * jax.nn.sigmoid(-z)
    o = hgrn2_chunked(qh, kh, heads(i), log_f)
    o = rms_norm(o, norm_g).reshape(B, S, D_HGRN)
    return (o * jax.nn.silu(g.astype(jnp.float32))).astype(q.dtype)


def pool_branch(u, pool_w, pool_scale):
    B, S, _ = u.shape
    uf = u.astype(jnp.float32).reshape(B, S, POOL_GROUPS, POOL_GROUP_DIM)
    cs = jnp.pad(jnp.cumsum(uf, axis=1), ((0, 0), (1, 0), (0, 0), (0, 0)))
    counts_base = jnp.arange(1, S + 1, dtype=jnp.float32)
    means = []
    for gi, w in enumerate(POOL_WINDOWS):
        csg = cs[:, :, gi]
        prev = jnp.pad(csg, ((0, 0), (w - 1, 0), (0, 0)))[:, :S]
        count = jnp.minimum(counts_base, float(w))
        means.append((csg[:, 1:] - prev) / count[None, :, None])
    pooled = jnp.stack(means, axis=2) - uf
    y = jnp.einsum('bsgd,gde->bsge', pooled, pool_w.astype(jnp.float32)).reshape(B, S, D_POOL)
    return (y * pool_scale.astype(jnp.float32)).astype(u.dtype)


def conv_glu(h, w_up, conv_w, conv_b, w_down):
    a, v = jnp.split(h @ w_up, 2, axis=-1)
    S = a.shape[1]
    ap = jnp.pad(a, ((0, 0), (CONV_WIDTH - 1, 0), (0, 0)))
    conv = conv_b
    for tap in range(CONV_WIDTH):
        conv = conv + conv_w[tap] * ap[:, tap:tap + S]
    return (jax.nn.silu(conv) * v) @ w_down


def setup_inputs(seed: int = 0):
    key = jax.random.key(seed)
    ks = jax.random.split(key, 17)
    f32 = jnp.float32

    def nrm(k, shape, scale):
        return jax.random.normal(k, shape, f32) * scale

    def gain(k, shape):
        return 1.0 + 0.02 * jax.random.normal(k, shape, f32)

    return {
        'x': nrm(ks[0], (BATCH, SEQ, D_MODEL), 1.0),
        'c': nrm(ks[1], (BATCH, D_MODEL), 1.0),
        'ada_w': nrm(ks[2], (DEPTH, D_MODEL, N_MOD * D_MODEL), 0.5 * D_MODEL ** -0.5),
        'ada_b': nrm(ks[3], (DEPTH, N_MOD * D_MODEL), 0.01),
        'mix_norm_g': gain(ks[4], (DEPTH, D_MODEL)),
        'w_in': nrm(ks[5], (DEPTH, D_MODEL, D_IN), D_MODEL ** -0.5),
        'hgrn_lower_bounds': nrm(ks[6], (DEPTH, D_HGRN), 0.1),
        'hgrn_norm_g': gain(ks[7], (DEPTH, HGRN_HEAD_DIM)),
        'pool_w': nrm(ks[8], (DEPTH, POOL_GROUPS, POOL_GROUP_DIM, POOL_GROUP_DIM), POOL_GROUP_DIM ** -0.5),
        'pool_scale': gain(ks[9], (DEPTH, D_POOL)),
        'w_out': nrm(ks[10], (DEPTH, D_MIX, D_MODEL), D_MIX ** -0.5),
        'ffn_norm_g': gain(ks[11], (DEPTH, D_MODEL)),
        'w_up': nrm(ks[12], (DEPTH, D_MODEL, 2 * D_FF), D_MODEL ** -0.5),
        'conv_w': nrm(ks[13], (DEPTH, CONV_WIDTH, D_FF), CONV_WIDTH ** -0.5),
        'conv_b': nrm(ks[14], (DEPTH, D_FF), 0.01),
        'w_down': nrm(ks[15], (DEPTH, D_FF, D_MODEL), D_FF ** -0.5),
        'final_norm_g': gain(ks[16], (D_MODEL,)),
    }


def reference(x, c, ada_w, ada_b, mix_norm_g, w_in, hgrn_lower_bounds, hgrn_norm_g, pool_w,
              pool_scale, w_out, ffn_norm_g, w_up, conv_w, conv_b, w_down, final_norm_g):
    p = jax.nn.softmax(hgrn_lower_bounds.astype(jnp.float32), axis=0)
    lower_bounds = jnp.clip(jnp.cumsum(p, axis=0) - p[0:1], 0.0, 1.0)
    c_act = jax.nn.silu(c)
    for l in range(DEPTH):
        mod = (c_act @ ada_w[l] + ada_b[l])[:, None, :]
        shift1, scale1, gate1, shift2, scale2, gate2 = jnp.split(mod, N_MOD, axis=-1)

        h = rms_norm(x, mix_norm_g[l]) * (1.0 + scale1) + shift1
        proj = h @ w_in[l]
        q, f, i, g, u = jnp.split(proj, [D_HGRN, 2 * D_HGRN, 3 * D_HGRN, 4 * D_HGRN], axis=-1)
        y_a = hgrn2_branch(q, f, i, g, lower_bounds[l], hgrn_norm_g[l])
        y_b = pool_branch(u, pool_w[l], pool_scale[l])
        x = x + gate1 * (jnp.concatenate([y_a, y_b], axis=-1) @ w_out[l])

        h = rms_norm(x, ffn_norm_g[l]) * (1.0 + scale2) + shift2
        x = x + gate2 * conv_glu(h, w_up[l], conv_w[l], conv_b[l], w_down[l])
    return rms_norm(x, final_norm_g)
```

```python
import functools

import jax
import jax.numpy as jnp
import numpy as np
from jax import lax
from jax.experimental import pallas as pl
from jax.experimental.pallas import tpu as pltpu

F32 = jnp.float32
BF16 = jnp.bfloat16

LANES = 128
HEAD_DIM = 128
POOL_WINDOWS = (2, 4, 8, 16)
POOL_HALO = 128
CONV_WIDTH = 3
N_MOD = 6
EPS = 1e-6
VMEM_LIMIT = 56 * 1024 * 1024

HGRN_CHUNK = 128
HGRN_TILE = 512
POOL_TILE = 1024
INPROJ_TM, INPROJ_TN = 1024, 1024
OUTPROJ_TM = 512
FFN_TM, FFN_TF = 512, 512
NORM_TM = 1024
MOD_TN = 1024


def _params(n_axes, vmem=VMEM_LIMIT):
    return pltpu.CompilerParams(
        dimension_semantics=("arbitrary",) * n_axes, vmem_limit_bytes=vmem)


def _norm_mod(xf, gain, scale, shift):
    ms = jnp.mean(xf * xf, axis=-1, keepdims=True)
    y = xf * lax.rsqrt(ms + EPS) * gain
    return y * (1.0 + scale) + shift


def _sigmoid(x):
    return 1.0 / (1.0 + jnp.exp(-x))


def _mod_kernel(c_ref, w_ref, b_ref, o_ref):
    c = c_ref[...]
    c_act = (c * _sigmoid(c)).astype(BF16)
    acc = jnp.dot(c_act, w_ref[...].astype(BF16), preferred_element_type=F32)
    o_ref[...] = acc + b_ref[...]


def _modulation(c, ada_w, ada_b):
    depth, d, n = ada_w.shape
    b = c.shape[0]
    return pl.pallas_call(
        _mod_kernel,
        grid=(depth, n // MOD_TN),
        in_specs=[
            pl.BlockSpec((b, d), lambda l, j: (0, 0)),
            pl.BlockSpec((None, d, MOD_TN), lambda l, j: (l, 0, j)),
            pl.BlockSpec((None, 1, MOD_TN), lambda l, j: (l, 0, j)),
        ],
        out_specs=pl.BlockSpec((None, b, MOD_TN), lambda l, j: (l, 0, j)),
        out_shape=jax.ShapeDtypeStruct((depth, b, n), F32),
        compiler_params=_params(2),
        name="ada_mod",
    )(c, ada_w, ada_b.reshape(depth, 1, n))


def _inproj_kernel(l_ref, x_ref, g_ref, sh_ref, sc_ref, w_ref, o_ref, h_ref):
    @pl.when(pl.program_id(1) == 0)
    def _():
        h_ref[...] = _norm_mod(x_ref[...], g_ref[...], sc_ref[...], sh_ref[...]).astype(BF16)

    acc = jnp.dot(h_ref[...], w_ref[...], preferred_element_type=F32)
    for blk in range(o_ref.shape[0]):
        o_ref[blk] = acc[:, blk * LANES:(blk + 1) * LANES].astype(BF16)


def _in_proj(l, x2d, gains, mods, w_in, seq):
    t, d = x2d.shape
    n = w_in.shape[2]
    tm, tn = INPROJ_TM, INPROJ_TN
    per_seq = seq // tm
    grid_spec = pltpu.PrefetchScalarGridSpec(
        num_scalar_prefetch=1,
        grid=(t // tm, n // tn),
        in_specs=[
            pl.BlockSpec((tm, d), lambda i, j, l: (i, 0)),
            pl.BlockSpec((None, 1, d), lambda i, j, l: (l[0], 0, 0)),
            pl.BlockSpec((None, None, None, 1, d), lambda i, j, l: (l[0], i // per_seq, 0, 0, 0)),
            pl.BlockSpec((None, None, None, 1, d), lambda i, j, l: (l[0], i // per_seq, 1, 0, 0)),
            pl.BlockSpec((None, d, tn), lambda i, j, l: (l[0], 0, j)),
        ],
        out_specs=pl.BlockSpec((tn // LANES, tm, LANES), lambda i, j, l: (j, i, 0)),
        scratch_shapes=[pltpu.VMEM((tm, d), BF16)],
    )
    return pl.pallas_call(
        _inproj_kernel,
        grid_spec=grid_spec,
        out_shape=jax.ShapeDtypeStruct((n // LANES, t, LANES), BF16),
        compiler_params=_params(2),
        name="in_proj",
    )(l, x2d, gains, mods, mods, w_in)


def _hgrn_level_matrix(chunk):
    t = np.arange(chunk)[:, None]
    r = np.arange(chunk)[None, :]
    low = (r <= t).astype(np.float32)
    mats = [low]
    m = 1
    while m < chunk:
        anchor = (t // (2 * m)) * (2 * m) + m - 1
        mats.append(low - (r <= anchor).astype(np.float32))
        m *= 2
    stacked = np.concatenate(mats, axis=0)
    return np.concatenate([stacked] * 3, axis=1)


def _hgrn_kernel(l_ref, q_ref, f_ref, i_ref, g_ref, lb_ref, ng_ref, lm_ref, o_ref, st_ref):
    chunk = HGRN_CHUNK
    n_levels = chunk.bit_length() - 1
    n_heads, tile, _ = q_ref.shape
    layer = l_ref[0]

    @pl.when(pl.program_id(1) == 0)
    def _():
        st_ref[...] = jnp.zeros_like(st_ref)

    row = lax.broadcasted_iota(jnp.int32, (chunk, HEAD_DIM), 0)
    trow = lax.broadcasted_iota(jnp.int32, (chunk, chunk), 0)
    tcol = lax.broadcasted_iota(jnp.int32, (chunk, chunk), 1)
    txor = trow ^ tcol
    lower = trow > tcol

    def head_body(h, carry):
        lbraw = lb_ref[h]
        ex = jnp.exp(lbraw - jnp.max(lbraw, axis=0, keepdims=True))
        p = ex / jnp.sum(ex, axis=0, keepdims=True)
        lrow = lax.broadcasted_iota(jnp.int32, p.shape, 0)
        cum = jnp.sum(jnp.where(lrow <= layer, p, 0.0), axis=0, keepdims=True)
        lb = jnp.clip(cum - p[0:1], 0.0, 1.0)
        gain = ng_ref[...]

        def chunk_body(c, carry2):
            rows = pl.ds(pl.multiple_of(c * chunk, chunk), chunk)
            z = f_ref[h, rows, :].astype(F32)
            e = jnp.exp(-z)
            r = 1.0 / (1.0 + e)
            decay = (1.0 + lb * e) * r
            key = (1.0 - lb) * (e * r)
            logf = jnp.log(decay)
            qraw = q_ref[h, rows, :].astype(F32)
            qq = qraw * _sigmoid(qraw) * (HEAD_DIM ** -0.5)
            val = i_ref[h, rows, :]

            hi = logf.astype(BF16)
            r1 = logf - hi.astype(F32)
            mid = r1.astype(BF16)
            lo = (r1 - mid.astype(F32)).astype(BF16)
            parts = jnp.concatenate([hi, mid, lo], axis=0)
            bd = jnp.dot(lm_ref[...], parts, preferred_element_type=F32)
            b = bd[0:chunk]

            s0 = lax.dot_general(qq.astype(BF16), key.astype(BF16),
                                 (((1,), (1,)), ((), ())), preferred_element_type=F32)
            scores = jnp.where(trow == tcol, s0, 0.0)
            for k in range(n_levels):
                dk = bd[chunk * (k + 1):chunk * (k + 2)]
                w = jnp.exp(-jnp.abs(dk))
                odd = ((row >> k) & 1) == 1
                xk = (jnp.where(odd, qq, key) * w).astype(BF16)
                sk = lax.dot_general(xk, xk, (((1,), (1,)), ((), ())), preferred_element_type=F32)
                scores = jnp.where(((txor >> k) == 1) & lower, sk, scores)

            st = st_ref[h]
            q_in = (qq * jnp.exp(b)).astype(BF16)
            o = jnp.dot(scores.astype(BF16), val, preferred_element_type=F32)
            o = o + lax.dot_general(q_in, st.astype(BF16), (((1,), (1,)), ((), ())),
                                    preferred_element_type=F32)

            b_last = b[chunk - 1:chunk]
            k_out = (key * jnp.exp(b_last - b)).astype(BF16)
            upd = lax.dot_general(val, k_out, (((0,), (0,)), ((), ())), preferred_element_type=F32)
            st_ref[h] = st * jnp.exp(b_last) + upd

            ms = jnp.mean(o * o, axis=-1, keepdims=True)
            on = o * lax.rsqrt(ms + EPS) * gain
            graw = g_ref[h, rows, :].astype(F32)
            o_ref[h, rows, :] = (on * (graw * _sigmoid(graw))).astype(BF16)
            return carry2

        lax.fori_loop(0, tile // chunk, chunk_body, 0)
        return carry

    lax.fori_loop(0, n_heads, head_body, 0)


def _hgrn(l, proj, lbs, norm_g, level_mat, batch, seq):
    n_heads = lbs.shape[0]
    t = proj.shape[1]
    tile = HGRN_TILE
    per_seq = seq // tile
    proj5 = proj.reshape(5, n_heads, t, LANES)

    def sec_spec(sec):
        return pl.BlockSpec((None, n_heads, tile, LANES),
                            lambda b, s, l: (sec, 0, b * per_seq + s, 0))

    grid_spec = pltpu.PrefetchScalarGridSpec(
        num_scalar_prefetch=1,
        grid=(batch, per_seq),
        in_specs=[
            sec_spec(0), sec_spec(1), sec_spec(2), sec_spec(3),
            pl.BlockSpec(lbs.shape, lambda b, s, l: (0, 0, 0)),
            pl.BlockSpec((None, 1, HEAD_DIM), lambda b, s, l: (l[0], 0, 0)),
            pl.BlockSpec(level_mat.shape, lambda b, s, l: (0, 0)),
        ],
        out_specs=pl.BlockSpec((n_heads, tile, LANES), lambda b, s, l: (0, b * per_seq + s, 0)),
        scratch_shapes=[pltpu.VMEM((n_heads, HEAD_DIM, HEAD_DIM), F32)],
    )
    return pl.pallas_call(
        _hgrn_kernel,
        grid_spec=grid_spec,
        out_shape=jax.ShapeDtypeStruct((n_heads, t, LANES), BF16),
        compiler_params=_params(2),
        name="hgrn2",
    )(l, proj5, proj5, proj5, proj5, lbs, norm_g, level_mat)


def _pool_kernel(l_ref, u_ref, pw_ref, ps_ref, o_ref, ext_ref):
    grp = pl.program_id(1)
    s = pl.program_id(2)
    tile = u_ref.shape[1]
    width = jnp.left_shift(2, grp)

    @pl.when(s == 0)
    def _():
        ext_ref[0:POOL_HALO, :] = jnp.zeros((POOL_HALO, ext_ref.shape[1]), BF16)

    u = jnp.concatenate([u_ref[0], u_ref[1]], axis=1)
    ext_ref[POOL_HALO:, :] = u

    trow = lax.broadcasted_iota(jnp.int32, (tile, tile + POOL_HALO), 0) + POOL_HALO
    rcol = lax.broadcasted_iota(jnp.int32, (tile, tile + POOL_HALO), 1)
    band = ((rcol <= trow) & (rcol > trow - width)).astype(BF16)
    sums = jnp.dot(band, ext_ref[...], preferred_element_type=F32)

    pos = lax.broadcasted_iota(jnp.int32, (tile, 1), 0) + s * tile + 1
    count = jnp.minimum(pos, width).astype(F32)
    pooled = sums / count - u.astype(F32)
    y = jnp.dot(pooled.astype(BF16), pw_ref[...], preferred_element_type=F32)
    o_ref[...] = (y * ps_ref[...]).astype(BF16)
    ext_ref[0:POOL_HALO, :] = u[tile - POOL_HALO:, :]


def _pool(l, proj, pool_w, pool_scale, batch, seq):
    t = proj.shape[1]
    depth, groups, gd, _ = pool_w.shape
    tile = POOL_TILE
    per_seq = seq // tile
    blocks_per_group = gd // LANES
    first_block = proj.shape[0] - groups * blocks_per_group
    grid_spec = pltpu.PrefetchScalarGridSpec(
        num_scalar_prefetch=1,
        grid=(batch, groups, per_seq),
        in_specs=[
            pl.BlockSpec((blocks_per_group, tile, LANES),
                         lambda b, g, s, l: (first_block // blocks_per_group + g, b * per_seq + s, 0)),
            pl.BlockSpec((None, None, gd, gd), lambda b, g, s, l: (l[0], g, 0, 0)),
            pl.BlockSpec((None, None, 1, gd), lambda b, g, s, l: (l[0], g, 0, 0)),
        ],
        out_specs=pl.BlockSpec((tile, gd), lambda b, g, s, l: (b * per_seq + s, g)),
        scratch_shapes=[pltpu.VMEM((tile + POOL_HALO, gd), BF16)],
    )
    return pl.pallas_call(
        _pool_kernel,
        grid_spec=grid_spec,
        out_shape=jax.ShapeDtypeStruct((t, groups * gd), BF16),
        compiler_params=_params(3),
        name="pool_mix",
    )(l, proj, pool_w, pool_scale.reshape(depth, groups, 1, gd))


def _outproj_kernel(l_ref, ya_ref, yb_ref, w_ref, x_ref, gate_ref, o_ref):
    y = jnp.concatenate([ya_ref[h] for h in range(ya_ref.shape[0])] + [yb_ref[...]], axis=1)
    acc = jnp.dot(y, w_ref[...], preferred_element_type=F32)
    o_ref[...] = x_ref[...] + gate_ref[...] * acc


def _out_proj(l, ya, yb, w_out, x2d, mods, seq):
    t, d = x2d.shape
    n_heads = ya.shape[0]
    tm = OUTPROJ_TM
    per_seq = seq // tm
    grid_spec = pltpu.PrefetchScalarGridSpec(
        num_scalar_prefetch=1,
        grid=(t // tm,),
        in_specs=[
            pl.BlockSpec((n_heads, tm, LANES), lambda i, l: (0, i, 0)),
            pl.BlockSpec((tm, yb.shape[1]), lambda i, l: (i, 0)),
            pl.BlockSpec((None, d, d), lambda i, l: (l[0], 0, 0)),
            pl.BlockSpec((tm, d), lambda i, l: (i, 0)),
            pl.BlockSpec((None, None, None, 1, d), lambda i, l: (l[0], i // per_seq, 2, 0, 0)),
        ],
        out_specs=pl.BlockSpec((tm, d), lambda i, l: (i, 0)),
    )
    return pl.pallas_call(
        _outproj_kernel,
        grid_spec=grid_spec,
        out_shape=jax.ShapeDtypeStruct((t, d), F32),
        compiler_params=_params(1),
        name="out_proj",
    )(l, ya, yb, w_out, x2d, mods)


def _shift_rows(a, k, halo):
    rolled = pltpu.roll(a, k, axis=0)
    prev = pltpu.roll(halo, k, axis=0)
    ri = lax.broadcasted_iota(jnp.int32, halo.shape, 0)
    top = jnp.where(ri < k, prev, rolled[0:halo.shape[0]])
    return jnp.concatenate([top, rolled[halo.shape[0]:]], axis=0)


def _ffn_kernel(l_ref, x_ref, g_ref, sh_ref, sc_ref, gate_ref, wa_ref, wv_ref, cw_ref, cb_ref, wd_ref,
                o_ref, h_ref, acc_ref, halo_ref, *, tiles_per_seq):
    i = pl.program_id(0)
    j = pl.program_id(1)
    halo_rows = halo_ref.shape[1]

    @pl.when(j == 0)
    def _():
        h_ref[...] = _norm_mod(x_ref[...], g_ref[...], sc_ref[...], sh_ref[...]).astype(BF16)
        acc_ref[...] = jnp.zeros_like(acc_ref)

    @pl.when(i % tiles_per_seq == 0)
    def _():
        halo_ref[j] = jnp.zeros(halo_ref.shape[1:], F32)

    h = h_ref[...]
    a = jnp.dot(h, wa_ref[...], preferred_element_type=F32)
    v = jnp.dot(h, wv_ref[...], preferred_element_type=F32)
    halo = halo_ref[j]
    cw = cw_ref[...]
    conv = cb_ref[...] + cw[CONV_WIDTH - 1:CONV_WIDTH] * a
    for back in range(1, CONV_WIDTH):
        tap = CONV_WIDTH - 1 - back
        conv = conv + cw[tap:tap + 1] * _shift_rows(a, back, halo)
    halo_ref[j] = a[a.shape[0] - halo_rows:, :]
    glu = (conv * _sigmoid(conv) * v).astype(BF16)
    acc_ref[...] += jnp.dot(glu, wd_ref[...], preferred_element_type=F32)

    @pl.when(j == pl.num_programs(1) - 1)
    def _():
        o_ref[...] = x_ref[...] + gate_ref[...] * acc_ref[...]


def _ffn(l, x2d, gains, mods, w_up, conv_w, conv_b, w_down, seq):
    t, d = x2d.shape
    depth, ff, _ = w_down.shape
    tm, tf = FFN_TM, FFN_TF
    per_seq = seq // tm
    n_ff = ff // tf

    def mod_spec(k):
        return pl.BlockSpec((None, None, None, 1, d), lambda i, j, l: (l[0], i // per_seq, k, 0, 0))

    grid_spec = pltpu.PrefetchScalarGridSpec(
        num_scalar_prefetch=1,
        grid=(t // tm, n_ff),
        in_specs=[
            pl.BlockSpec((tm, d), lambda i, j, l: (i, 0)),
            pl.BlockSpec((None, 1, d), lambda i, j, l: (l[0], 0, 0)),
            mod_spec(3), mod_spec(4), mod_spec(5),
            pl.BlockSpec((None, d, tf), lambda i, j, l: (l[0], 0, j)),
            pl.BlockSpec((None, d, tf), lambda i, j, l: (l[0], 0, n_ff + j)),
            pl.BlockSpec((None, CONV_WIDTH, tf), lambda i, j, l: (l[0], 0, j)),
            pl.BlockSpec((None, 1, tf), lambda i, j, l: (l[0], 0, j)),
            pl.BlockSpec((None, tf, d), lambda i, j, l: (l[0], j, 0)),
        ],
        out_specs=pl.BlockSpec((tm, d), lambda i, j, l: (i, 0)),
        scratch_shapes=[
            pltpu.VMEM((tm, d), BF16),
            pltpu.VMEM((tm, d), F32),
            pltpu.VMEM((n_ff, 8, tf), F32),
        ],
    )
    return pl.pallas_call(
        functools.partial(_ffn_kernel, tiles_per_seq=per_seq),
        grid_spec=grid_spec,
        out_shape=jax.ShapeDtypeStruct((t, d), F32),
        compiler_params=_params(2),
        name="conv_glu_ffn",
    )(l, x2d, gains, mods, mods, mods, w_up, w_up, conv_w, conv_b.reshape(depth, 1, ff), w_down)


def _final_norm_kernel(x_ref, g_ref, o_ref):
    xf = x_ref[...]
    ms = jnp.mean(xf * xf, axis=-1, keepdims=True)
    o_ref[...] = xf * lax.rsqrt(ms + EPS) * g_ref[...]


def _final_norm(x2d, gain):
    t, d = x2d.shape
    return pl.pallas_call(
        _final_norm_kernel,
        grid=(t // NORM_TM,),
        in_specs=[pl.BlockSpec((NORM_TM, d), lambda i: (i, 0)),
                  pl.BlockSpec((1, d), lambda i: (0, 0))],
        out_specs=pl.BlockSpec((NORM_TM, d), lambda i: (i, 0)),
        out_shape=jax.ShapeDtypeStruct((t, d), F32),
        compiler_params=_params(1),
        name="final_norm",
    )(x2d, gain.reshape(1, d))


def kernel(x, c, ada_w, ada_b, mix_norm_g, w_in, hgrn_lower_bounds, hgrn_norm_g, pool_w, pool_scale,
           w_out, ffn_norm_g, w_up, conv_w, conv_b, w_down, final_norm_g):
    batch, seq, d = x.shape
    depth = w_in.shape[0]
    d_hgrn = hgrn_lower_bounds.shape[1]
    n_heads = d_hgrn // HEAD_DIM

    mods = _modulation(c, ada_w, ada_b).reshape(depth, batch, N_MOD, 1, d)
    w_in_b, w_out_b = w_in.astype(BF16), w_out.astype(BF16)
    w_up_b, w_down_b, pool_w_b = w_up.astype(BF16), w_down.astype(BF16), pool_w.astype(BF16)
    lbs = hgrn_lower_bounds.reshape(depth, n_heads, HEAD_DIM).transpose(1, 0, 2)
    mix_g = mix_norm_g.reshape(depth, 1, d)
    ffn_g = ffn_norm_g.reshape(depth, 1, d)
    head_g = hgrn_norm_g.reshape(depth, 1, HEAD_DIM)
    level_mat = jnp.asarray(_hgrn_level_matrix(HGRN_CHUNK), dtype=BF16)

    def layer(li, xc):
        l = jnp.full((1,), li, jnp.int32)
        proj = _in_proj(l, xc, mix_g, mods, w_in_b, seq)
        ya = _hgrn(l, proj, lbs, head_g, level_mat, batch, seq)
        yb = _pool(l, proj, pool_w_b, pool_scale, batch, seq)
        x1 = _out_proj(l, ya, yb, w_out_b, xc, mods, seq)
        return _ffn(l, x1, ffn_g, mods, w_up_b, conv_w, conv_b, w_down_b, seq)

    xf = lax.fori_loop(0, depth, layer, x.reshape(batch * seq, d))
    return _final_norm(xf, final_norm_g).reshape(batch, seq, d)
```

```python
import functools

import jax
import jax.numpy as jnp
import numpy as np
from jax import lax
from jax.experimental import pallas as pl
from jax.experimental.pallas import tpu as pltpu

F32 = jnp.float32
BF16 = jnp.bfloat16

LANES = 128
HEAD_DIM = 128
POOL_WINDOWS = (2, 4, 8, 16)
POOL_HALO = 128
CONV_WIDTH = 3
N_MOD = 6
EPS = 1e-6
VMEM_LIMIT = 56 * 1024 * 1024

HGRN_CHUNK = 128
HGRN_TILE = 512
POOL_TILE = 1024
INPROJ_TM, INPROJ_TN = 1024, 1024
OUTPROJ_TM = 512
FFN_TM, FFN_TF = 512, 512
NORM_TM = 1024
MOD_TN = 1024


def _params(n_axes, vmem=VMEM_LIMIT):
    return pltpu.CompilerParams(
        dimension_semantics=("arbitrary",) * n_axes, vmem_limit_bytes=vmem)


def _norm_mod(xf, gain, scale, shift):
    ms = jnp.mean(xf * xf, axis=-1, keepdims=True)
    y = xf * lax.rsqrt(ms + EPS) * gain
    return y * (1.0 + scale) + shift


def _sigmoid(x):
    return 1.0 / (1.0 + jnp.exp(-x))


def _mod_kernel(c_ref, w_ref, b_ref, o_ref):
    c = c_ref[...]
    c_act = (c * _sigmoid(c)).astype(BF16)
    acc = jnp.dot(c_act, w_ref[...].astype(BF16), preferred_element_type=F32)
    o_ref[...] = acc + b_ref[...]


def _modulation(c, ada_w, ada_b):
    depth, d, n = ada_w.shape
    b = c.shape[0]
    return pl.pallas_call(
        _mod_kernel,
        grid=(depth, n // MOD_TN),
        in_specs=[
            pl.BlockSpec((b, d), lambda l, j: (0, 0)),
            pl.BlockSpec((None, d, MOD_TN), lambda l, j: (l, 0, j)),
            pl.BlockSpec((None, 1, MOD_TN), lambda l, j: (l, 0, j)),
        ],
        out_specs=pl.BlockSpec((None, b, MOD_TN), lambda l, j: (l, 0, j)),
        out_shape=jax.ShapeDtypeStruct((depth, b, n), F32),
        compiler_params=_params(2),
        name="ada_mod",
    )(c, ada_w, ada_b.reshape(depth, 1, n))


def _inproj_kernel(l_ref, x_ref, g_ref, sh_ref, sc_ref, w_ref, o_ref, h_ref):
    @pl.when(pl.program_id(1) == 0)
    def _():
        h_ref[...] = _norm_mod(x_ref[...], g_ref[...], sc_ref[...], sh_ref[...]).astype(BF16)

    acc = jnp.dot(h_ref[...], w_ref[...], preferred_element_type=F32)
    for blk in range(o_ref.shape[0]):
        o_ref[blk] = acc[:, blk * LANES:(blk + 1) * LANES].astype(BF16)


def _in_proj(l, x2d, gains, mods, w_in, seq):
    t, d = x2d.shape
    n = w_in.shape[2]
    tm, tn = INPROJ_TM, INPROJ_TN
    per_seq = seq // tm
    grid_spec = pltpu.PrefetchScalarGridSpec(
        num_scalar_prefetch=1,
        grid=(t // tm, n // tn),
        in_specs=[
            pl.BlockSpec((tm, d), lambda i, j, l: (i, 0)),
            pl.BlockSpec((None, 1, d), lambda i, j, l: (l[0], 0, 0)),
            pl.BlockSpec((None, None, None, 1, d), lambda i, j, l: (l[0], i // per_seq, 0, 0, 0)),
            pl.BlockSpec((None, None, None, 1, d), lambda i, j, l: (l[0], i // per_seq, 1, 0, 0)),
            pl.BlockSpec((None, d, tn), lambda i, j, l: (l[0], 0, j)),
        ],
        out_specs=pl.BlockSpec((tn // LANES, tm, LANES), lambda i, j, l: (j, i, 0)),
        scratch_shapes=[pltpu.VMEM((tm, d), BF16)],
    )
    return pl.pallas_call(
        _inproj_kernel,
        grid_spec=grid_spec,
        out_shape=jax.ShapeDtypeStruct((n // LANES, t, LANES), BF16),
        compiler_params=_params(2),
        name="in_proj",
    )(l, x2d, gains, mods, mods, w_in)


def _hgrn_level_matrix(chunk):
    t = np.arange(chunk)[:, None]
    r = np.arange(chunk)[None, :]
    low = (r <= t).astype(np.float32)
    mats = [low]
    m = 1
    while m < chunk:
        anchor = (t // (2 * m)) * (2 * m) + m - 1
        mats.append(low - (r <= anchor).astype(np.float32))
        m *= 2
    stacked = np.concatenate(mats, axis=0)
    return np.concatenate([stacked] * 3, axis=1)


_NT = (((1,), (1,)), ((), ()))
_TN = (((0,), (0,)), ((), ()))


def _hgrn_kernel(l_ref, q_ref, f_ref, i_ref, g_ref, lb_ref, ng_ref, lm_ref, o_ref, st_ref, lvl_ref):
    chunk = HGRN_CHUNK
    n_levels = chunk.bit_length() - 1
    n_heads, tile, _ = q_ref.shape
    n_chunks = tile // chunk
    layer = l_ref[0]

    @pl.when((pl.program_id(0) == 0) & (pl.program_id(1) == 0))
    def _():
        trow = lax.broadcasted_iota(jnp.int32, (chunk, chunk), 0)
        tcol = lax.broadcasted_iota(jnp.int32, (chunk, chunk), 1)
        txor = trow ^ tcol
        top = jnp.full((chunk, chunk), -1, jnp.int32)
        for k in range(n_levels):
            top = top + ((txor >> k) != 0).astype(jnp.int32)
        lvl_ref[...] = jnp.where(trow > tcol, top, jnp.where(trow == tcol, n_levels, -1))

    @pl.when(pl.program_id(1) == 0)
    def _():
        st_ref[...] = jnp.zeros_like(st_ref)

    def head_body(h, carry):
        lbraw = lb_ref[h]
        ex = jnp.exp(lbraw - jnp.max(lbraw, axis=0, keepdims=True))
        p = ex / jnp.sum(ex, axis=0, keepdims=True)
        lrow = lax.broadcasted_iota(jnp.int32, p.shape, 0)
        cum = jnp.sum(jnp.where(lrow <= layer, p, 0.0), axis=0, keepdims=True)
        lb = jnp.clip(cum - p[0:1], 0.0, 1.0)
        gain = ng_ref[...]

        qs, keys, parts = [], [], []
        for c in range(n_chunks):
            rows = pl.ds(c * chunk, chunk)
            z = f_ref[h, rows, :].astype(F32)
            e = jnp.exp(-z)
            r = 1.0 / (1.0 + e)
            decay = (1.0 + lb * e) * r
            keys.append((1.0 - lb) * (e * r))
            logf = jnp.log(decay)
            qraw = q_ref[h, rows, :].astype(F32)
            qs.append(qraw * _sigmoid(qraw) * (HEAD_DIM ** -0.5))
            hi = logf.astype(BF16)
            r1 = logf - hi.astype(F32)
            mid = r1.astype(BF16)
            lo = (r1 - mid.astype(F32)).astype(BF16)
            parts.append(jnp.concatenate([hi, mid, lo], axis=0))

        bd = jnp.dot(lm_ref[...], jnp.concatenate(parts, axis=1), preferred_element_type=F32)

        st = st_ref[h]
        for c in range(n_chunks):
            rows = pl.ds(c * chunk, chunk)
            cols = slice(c * HEAD_DIM, (c + 1) * HEAD_DIM)
            qq, key = qs[c], keys[c]
            val = i_ref[h, rows, :]
            b = bd[0:chunk, cols]

            lvl = lvl_ref[...]
            s0 = lax.dot_general(qq.astype(BF16), key.astype(BF16), _NT, preferred_element_type=F32)
            scores = jnp.where(lvl == n_levels, s0, 0.0)
            for k in range(n_levels):
                w = jnp.exp(-jnp.abs(bd[chunk * (k + 1):chunk * (k + 2), cols]))
                sk = lax.dot_general((qq * w).astype(BF16), (key * w).astype(BF16), _NT,
                                     preferred_element_type=F32)
                scores = jnp.where(lvl == k, sk, scores)

            q_in = (qq * jnp.exp(b)).astype(BF16)
            o = jnp.dot(scores.astype(BF16), val, preferred_element_type=F32)
            o = o + lax.dot_general(q_in, st.astype(BF16), _NT, preferred_element_type=F32)

            b_last = b[chunk - 1:chunk]
            k_out = (key * jnp.exp(b_last - b)).astype(BF16)
            st = st * jnp.exp(b_last) + lax.dot_general(val, k_out, _TN, preferred_element_type=F32)

            ms = jnp.mean(o * o, axis=-1, keepdims=True)
            on = o * lax.rsqrt(ms + EPS) * gain
            graw = g_ref[h, rows, :].astype(F32)
            o_ref[h, rows, :] = (on * (graw * _sigmoid(graw))).astype(BF16)
        st_ref[h] = st
        return carry

    lax.fori_loop(0, n_heads, head_body, 0)


def _hgrn(l, proj, lbs, norm_g, level_mat, batch, seq):
    n_heads = lbs.shape[0]
    t = proj.shape[1]
    tile = HGRN_TILE
    per_seq = seq // tile
    proj5 = proj.reshape(5, n_heads, t, LANES)

    def sec_spec(sec):
        return pl.BlockSpec((None, n_heads, tile, LANES),
                            lambda b, s, l: (sec, 0, b * per_seq + s, 0))

    grid_spec = pltpu.PrefetchScalarGridSpec(
        num_scalar_prefetch=1,
        grid=(batch, per_seq),
        in_specs=[
            sec_spec(0), sec_spec(1), sec_spec(2), sec_spec(3),
            pl.BlockSpec(lbs.shape, lambda b, s, l: (0, 0, 0)),
            pl.BlockSpec((None, 1, HEAD_DIM), lambda b, s, l: (l[0], 0, 0)),
            pl.BlockSpec(level_mat.shape, lambda b, s, l: (0, 0)),
        ],
        out_specs=pl.BlockSpec((n_heads, tile, LANES), lambda b, s, l: (0, b * per_seq + s, 0)),
        scratch_shapes=[pltpu.VMEM((n_heads, HEAD_DIM, HEAD_DIM), F32),
                        pltpu.VMEM((HGRN_CHUNK, HGRN_CHUNK), jnp.int32)],
    )
    return pl.pallas_call(
        _hgrn_kernel,
        grid_spec=grid_spec,
        out_shape=jax.ShapeDtypeStruct((n_heads, t, LANES), BF16),
        compiler_params=_params(2),
        name="hgrn2",
    )(l, proj5, proj5, proj5, proj5, lbs, norm_g, level_mat)


def _pool_kernel(l_ref, u_ref, pw_ref, ps_ref, o_ref, ext_ref):
    grp = pl.program_id(1)
    s = pl.program_id(2)
    tile = u_ref.shape[1]
    width = jnp.left_shift(2, grp)

    @pl.when(s == 0)
    def _():
        ext_ref[0:POOL_HALO, :] = jnp.zeros((POOL_HALO, ext_ref.shape[1]), BF16)

    u = jnp.concatenate([u_ref[0], u_ref[1]], axis=1)
    ext_ref[POOL_HALO:, :] = u

    trow = lax.broadcasted_iota(jnp.int32, (tile, tile + POOL_HALO), 0) + POOL_HALO
    rcol = lax.broadcasted_iota(jnp.int32, (tile, tile + POOL_HALO), 1)
    band = ((rcol <= trow) & (rcol > trow - width)).astype(BF16)
    sums = jnp.dot(band, ext_ref[...], preferred_element_type=F32)

    pos = lax.broadcasted_iota(jnp.int32, (tile, 1), 0) + s * tile + 1
    count = jnp.minimum(pos, width).astype(F32)
    pooled = sums / count - u.astype(F32)
    y = jnp.dot(pooled.astype(BF16), pw_ref[...], preferred_element_type=F32)
    o_ref[...] = (y * ps_ref[...]).astype(BF16)
    ext_ref[0:POOL_HALO, :] = u[tile - POOL_HALO:, :]


def _pool(l, proj, pool_w, pool_scale, batch, seq):
    t = proj.shape[1]
    depth, groups, gd, _ = pool_w.shape
    tile = POOL_TILE
    per_seq = seq // tile
    blocks_per_group = gd // LANES
    first_block = proj.shape[0] - groups * blocks_per_group
    grid_spec = pltpu.PrefetchScalarGridSpec(
        num_scalar_prefetch=1,
        grid=(batch, groups, per_seq),
        in_specs=[
            pl.BlockSpec((blocks_per_group, tile, LANES),
                         lambda b, g, s, l: (first_block // blocks_per_group + g, b * per_seq + s, 0)),
            pl.BlockSpec((None, None, gd, gd), lambda b, g, s, l: (l[0], g, 0, 0)),
            pl.BlockSpec((None, None, 1, gd), lambda b, g, s, l: (l[0], g, 0, 0)),
        ],
        out_specs=pl.BlockSpec((tile, gd), lambda b, g, s, l: (b * per_seq + s, g)),
        scratch_shapes=[pltpu.VMEM((tile + POOL_HALO, gd), BF16)],
    )
    return pl.pallas_call(
        _pool_kernel,
        grid_spec=grid_spec,
        out_shape=jax.ShapeDtypeStruct((t, groups * gd), BF16),
        compiler_params=_params(3),
        name="pool_mix",
    )(l, proj, pool_w, pool_scale.reshape(depth, groups, 1, gd))


def _outproj_kernel(l_ref, ya_ref, yb_ref, w_ref, x_ref, gate_ref, o_ref):
    y = jnp.concatenate([ya_ref[h] for h in range(ya_ref.shape[0])] + [yb_ref[...]], axis=1)
    acc = jnp.dot(y, w_ref[...], preferred_element_type=F32)
    o_ref[...] = x_ref[...] + gate_ref[...] * acc


def _out_proj(l, ya, yb, w_out, x2d, mods, seq):
    t, d = x2d.shape
    n_heads = ya.shape[0]
    tm = OUTPROJ_TM
    per_seq = seq // tm
    grid_spec = pltpu.PrefetchScalarGridSpec(
        num_scalar_prefetch=1,
        grid=(t // tm,),
        in_specs=[
            pl.BlockSpec((n_heads, tm, LANES), lambda i, l: (0, i, 0)),
            pl.BlockSpec((tm, yb.shape[1]), lambda i, l: (i, 0)),
            pl.BlockSpec((None, d, d), lambda i, l: (l[0], 0, 0)),
            pl.BlockSpec((tm, d), lambda i, l: (i, 0)),
            pl.BlockSpec((None, None, None, 1, d), lambda i, l: (l[0], i // per_seq, 2, 0, 0)),
        ],
        out_specs=pl.BlockSpec((tm, d), lambda i, l: (i, 0)),
    )
    return pl.pallas_call(
        _outproj_kernel,
        grid_spec=grid_spec,
        out_shape=jax.ShapeDtypeStruct((t, d), F32),
        compiler_params=_params(1),
        name="out_proj",
    )(l, ya, yb, w_out, x2d, mods)


def _shift_rows(a, k, halo):
    rolled = pltpu.roll(a, k, axis=0)
    prev = pltpu.roll(halo, k, axis=0)
    ri = lax.broadcasted_iota(jnp.int32, halo.shape, 0)
    top = jnp.where(ri < k, prev, rolled[0:halo.shape[0]])
    return jnp.concatenate([top, rolled[halo.shape[0]:]], axis=0)


def _ffn_kernel(l_ref, x_ref, g_ref, sh_ref, sc_ref, gate_ref, wa_ref, wv_ref, cw_ref, cb_ref, wd_ref,
                o_ref, h_ref, acc_ref, halo_ref, *, tiles_per_seq):
    i = pl.program_id(0)
    j = pl.program_id(1)
    halo_rows = halo_ref.shape[1]

    @pl.when(j == 0)
    def _():
        h_ref[...] = _norm_mod(x_ref[...], g_ref[...], sc_ref[...], sh_ref[...]).astype(BF16)
        acc_ref[...] = jnp.zeros_like(acc_ref)

    @pl.when(i % tiles_per_seq == 0)
    def _():
        halo_ref[j] = jnp.zeros(halo_ref.shape[1:], F32)

    h = h_ref[...]
    a = jnp.dot(h, wa_ref[...], preferred_element_type=F32)
    v = jnp.dot(h, wv_ref[...], preferred_element_type=F32)
    halo = halo_ref[j]
    cw = cw_ref[...]
    conv = cb_ref[...] + cw[CONV_WIDTH - 1:CONV_WIDTH] * a
    for back in range(1, CONV_WIDTH):
        tap = CONV_WIDTH - 1 - back
        conv = conv + cw[tap:tap + 1] * _shift_rows(a, back, halo)
    halo_ref[j] = a[a.shape[0] - halo_rows:, :]
    glu = (conv * _sigmoid(conv) * v).astype(BF16)
    acc_ref[...] += jnp.dot(glu, wd_ref[...], preferred_element_type=F32)

    @pl.when(j == pl.num_programs(1) - 1)
    def _():
        o_ref[...] = x_ref[...] + gate_ref[...] * acc_ref[...]


def _ffn(l, x2d, gains, mods, w_up, conv_w, conv_b, w_down, seq):
    t, d = x2d.shape
    depth, ff, _ = w_down.shape
    tm, tf = FFN_TM, FFN_TF
    per_seq = seq // tm
    n_ff = ff // tf

    def mod_spec(k):
        return pl.BlockSpec((None, None, None, 1, d), lambda i, j, l: (l[0], i // per_seq, k, 0, 0))

    grid_spec = pltpu.PrefetchScalarGridSpec(
        num_scalar_prefetch=1,
        grid=(t // tm, n_ff),
        in_specs=[
            pl.BlockSpec((tm, d), lambda i, j, l: (i, 0)),
            pl.BlockSpec((None, 1, d), lambda i, j, l: (l[0], 0, 0)),
            mod_spec(3), mod_spec(4), mod_spec(5),
            pl.BlockSpec((None, d, tf), lambda i, j, l: (l[0], 0, j)),
            pl.BlockSpec((None, d, tf), lambda i, j, l: (l[0], 0, n_ff + j)),
            pl.BlockSpec((None, CONV_WIDTH, tf), lambda i, j, l: (l[0], 0, j)),
            pl.BlockSpec((None, 1, tf), lambda i, j, l: (l[0], 0, j)),
            pl.BlockSpec((None, tf, d), lambda i, j, l: (l[0], j, 0)),
        ],
        out_specs=pl.BlockSpec((tm, d), lambda i, j, l: (i, 0)),
        scratch_shapes=[
            pltpu.VMEM((tm, d), BF16),
            pltpu.VMEM((tm, d), F32),
            pltpu.VMEM((n_ff, 8, tf), F32),
        ],
    )
    return pl.pallas_call(
        functools.partial(_ffn_kernel, tiles_per_seq=per_seq),
        grid_spec=grid_spec,
        out_shape=jax.ShapeDtypeStruct((t, d), F32),
        compiler_params=_params(2),
        name="conv_glu_ffn",
    )(l, x2d, gains, mods, mods, mods, w_up, w_up, conv_w, conv_b.reshape(depth, 1, ff), w_down)


def _final_norm_kernel(x_ref, g_ref, o_ref):
    xf = x_ref[...]
    ms = jnp.mean(xf * xf, axis=-1, keepdims=True)
    o_ref[...] = xf * lax.rsqrt(ms + EPS) * g_ref[...]


def _final_norm(x2d, gain):
    t, d = x2d.shape
    return pl.pallas_call(
        _final_norm_kernel,
        grid=(t // NORM_TM,),
        in_specs=[pl.BlockSpec((NORM_TM, d), lambda i: (i, 0)),
                  pl.BlockSpec((1, d), lambda i: (0, 0))],
        out_specs=pl.BlockSpec((NORM_TM, d), lambda i: (i, 0)),
        out_shape=jax.ShapeDtypeStruct((t, d), F32),
        compiler_params=_params(1),
        name="final_norm",
    )(x2d, gain.reshape(1, d))


def kernel(x, c, ada_w, ada_b, mix_norm_g, w_in, hgrn_lower_bounds, hgrn_norm_g, pool_w, pool_scale,
           w_out, ffn_norm_g, w_up, conv_w, conv_b, w_down, final_norm_g):
    batch, seq, d = x.shape
    depth = w_in.shape[0]
    d_hgrn = hgrn_lower_bounds.shape[1]
    n_heads = d_hgrn // HEAD_DIM

    mods = _modulation(c, ada_w, ada_b).reshape(depth, batch, N_MOD, 1, d)
    w_in_b, w_out_b = w_in.astype(BF16), w_out.astype(BF16)
    w_up_b, w_down_b, pool_w_b = w_up.astype(BF16), w_down.astype(BF16), pool_w.astype(BF16)
    lbs = hgrn_lower_bounds.reshape(depth, n_heads, HEAD_DIM).transpose(1, 0, 2)
    mix_g = mix_norm_g.reshape(depth, 1, d)
    ffn_g = ffn_norm_g.reshape(depth, 1, d)
    head_g = hgrn_norm_g.reshape(depth, 1, HEAD_DIM)
    level_mat = jnp.asarray(_hgrn_level_matrix(HGRN_CHUNK), dtype=BF16)

    xc = x.reshape(batch * seq, d)
    for li in range(depth):
        l = jnp.full((1,), li, jnp.int32)
        proj = _in_proj(l, xc, mix_g, mods, w_in_b, seq)
        ya = _hgrn(l, proj, lbs, head_g, level_mat, batch, seq)
        yb = _pool(l, proj, pool_w_b, pool_scale, batch, seq)
        x1 = _out_proj(l, ya, yb, w_out_b, xc, mods, seq)
        xc = _ffn(l, x1, ffn_g, mods, w_up_b, conv_w, conv_b, w_down_b, seq)
    return _final_norm(xc, final_norm_g).reshape(batch, seq, d)
```

```python
import functools

import jax
import jax.numpy as jnp
import numpy as np
from jax import lax
from jax.experimental import pallas as pl
from jax.experimental.pallas import tpu as pltpu

F32 = jnp.float32
BF16 = jnp.bfloat16

LANES = 128
HEAD_DIM = 128
POOL_WINDOWS = (2, 4, 8, 16)
POOL_HALO = 128
CONV_WIDTH = 3
N_MOD = 6
EPS = 1e-6
LOG2E = 1.4426950408889634
VMEM_LIMIT = 56 * 1024 * 1024

HGRN_CHUNK = 128
HGRN_TILE = 512
POOL_TILE = 1024
INPROJ_TM, INPROJ_TN = 1024, 1024
OUTPROJ_TM = 512
FFN_TM, FFN_TF = 512, 512
NORM_TM = 1024
MOD_TN = 1024


def _params(n_axes, vmem=VMEM_LIMIT):
    return pltpu.CompilerParams(
        dimension_semantics=("arbitrary",) * n_axes, vmem_limit_bytes=vmem)


def _norm_mod(xf, gain, scale, shift):
    ms = jnp.mean(xf * xf, axis=-1, keepdims=True)
    y = xf * lax.rsqrt(ms + EPS) * gain
    return y * (1.0 + scale) + shift


def _silu(x):
    return (0.5 * x) * (1.0 + jnp.tanh(0.5 * x))


def _mod_kernel(c_ref, w_ref, b_ref, o_ref):
    c = c_ref[...]
    c_act = _silu(c).astype(BF16)
    acc = jnp.dot(c_act, w_ref[...].astype(BF16), preferred_element_type=F32)
    o_ref[...] = acc + b_ref[...]


def _modulation(c, ada_w, ada_b):
    depth, d, n = ada_w.shape
    b = c.shape[0]
    return pl.pallas_call(
        _mod_kernel,
        grid=(depth, n // MOD_TN),
        in_specs=[
            pl.BlockSpec((b, d), lambda l, j: (0, 0)),
            pl.BlockSpec((None, d, MOD_TN), lambda l, j: (l, 0, j)),
            pl.BlockSpec((None, 1, MOD_TN), lambda l, j: (l, 0, j)),
        ],
        out_specs=pl.BlockSpec((None, b, MOD_TN), lambda l, j: (l, 0, j)),
        out_shape=jax.ShapeDtypeStruct((depth, b, n), F32),
        compiler_params=_params(2),
        name="ada_mod",
    )(c, ada_w, ada_b.reshape(depth, 1, n))


def _inproj_kernel(l_ref, x_ref, g_ref, sh_ref, sc_ref, w_ref, o_ref, h_ref):
    @pl.when(pl.program_id(1) == 0)
    def _():
        h_ref[...] = _norm_mod(x_ref[...], g_ref[...], sc_ref[...], sh_ref[...]).astype(BF16)

    acc = jnp.dot(h_ref[...], w_ref[...], preferred_element_type=F32)
    for blk in range(o_ref.shape[0]):
        o_ref[blk] = acc[:, blk * LANES:(blk + 1) * LANES].astype(BF16)


def _in_proj(l, x2d, gains, mods, w_in, seq):
    t, d = x2d.shape
    n = w_in.shape[2]
    tm, tn = INPROJ_TM, INPROJ_TN
    per_seq = seq // tm
    grid_spec = pltpu.PrefetchScalarGridSpec(
        num_scalar_prefetch=1,
        grid=(t // tm, n // tn),
        in_specs=[
            pl.BlockSpec((tm, d), lambda i, j, l: (i, 0)),
            pl.BlockSpec((None, 1, d), lambda i, j, l: (l[0], 0, 0)),
            pl.BlockSpec((None, None, None, 1, d), lambda i, j, l: (l[0], i // per_seq, 0, 0, 0)),
            pl.BlockSpec((None, None, None, 1, d), lambda i, j, l: (l[0], i // per_seq, 1, 0, 0)),
            pl.BlockSpec((None, d, tn), lambda i, j, l: (l[0], 0, j)),
        ],
        out_specs=pl.BlockSpec((tn // LANES, tm, LANES), lambda i, j, l: (j, i, 0)),
        scratch_shapes=[pltpu.VMEM((tm, d), BF16)],
    )
    return pl.pallas_call(
        _inproj_kernel,
        grid_spec=grid_spec,
        out_shape=jax.ShapeDtypeStruct((n // LANES, t, LANES), BF16),
        compiler_params=_params(2),
        name="in_proj",
    )(l, x2d, gains, mods, mods, w_in)


def _hgrn_level_matrix(chunk):
    t = np.arange(chunk)[:, None]
    r = np.arange(chunk)[None, :]
    low = (r <= t).astype(np.float32)
    mats = [low]
    m = 1
    while m < chunk:
        anchor = (t // (2 * m)) * (2 * m) + m - 1
        sign = np.where(t > anchor, 1.0, -1.0)
        mats.append(sign * (low - (r <= anchor).astype(np.float32)))
        m *= 2
    stacked = np.concatenate(mats, axis=0)
    return np.concatenate([stacked] * 2, axis=1)


_NT = (((1,), (1,)), ((), ()))
_TN = (((0,), (0,)), ((), ()))


def _hgrn_kernel(l_ref, q_ref, f_ref, i_ref, g_ref, lb_ref, ng_ref, lm_ref, o_ref,
                 st_ref, lvl_ref, qk0_ref, qk1_ref, bd0_ref, bd1_ref):
    chunk = HGRN_CHUNK
    n_levels = chunk.bit_length() - 1
    n_heads, tile, _ = q_ref.shape
    n_chunks = tile // chunk
    layer = l_ref[0]

    @pl.when((pl.program_id(0) == 0) & (pl.program_id(1) == 0))
    def _():
        trow = lax.broadcasted_iota(jnp.int32, (chunk, chunk), 0)
        tcol = lax.broadcasted_iota(jnp.int32, (chunk, chunk), 1)
        txor = trow ^ tcol
        top = jnp.full((chunk, chunk), -1, jnp.int32)
        for k in range(n_levels):
            top = top + ((txor >> k) != 0).astype(jnp.int32)
        lvl_ref[...] = jnp.where(trow > tcol, top, jnp.where(trow == tcol, n_levels, -1))

    @pl.when(pl.program_id(1) == 0)
    def _():
        st_ref[...] = jnp.zeros_like(st_ref)

    gain = ng_ref[...]

    def gates(h, qk_ref, bd_ref):
        lbraw = lb_ref[h]
        ex = jnp.exp(lbraw - jnp.max(lbraw, axis=0, keepdims=True))
        p = ex / jnp.sum(ex, axis=0, keepdims=True)
        lrow = lax.broadcasted_iota(jnp.int32, p.shape, 0)
        cum = jnp.sum(jnp.where(lrow <= layer, p, 0.0), axis=0, keepdims=True)
        lb = jnp.clip(cum - p[0:1], 0.0, 1.0)
        half_key = 0.5 * (1.0 - lb)

        parts = []
        for c in range(n_chunks):
            rows = pl.ds(c * chunk, chunk)
            z = f_ref[h, rows, :].astype(F32)
            e = jnp.exp2(z * (-LOG2E))
            logf = jnp.log2(1.0 + lb * e) - jnp.log2(1.0 + e)
            key = half_key - half_key * jnp.tanh(0.5 * z)
            qraw = q_ref[h, rows, :].astype(F32)
            qq = (qraw * (0.5 * HEAD_DIM ** -0.5)) * (1.0 + jnp.tanh(0.5 * qraw))
            qk_ref[0, rows, :] = qq.astype(BF16)
            qk_ref[1, rows, :] = key.astype(BF16)
            hi = logf.astype(BF16)
            lo = (logf - hi.astype(F32)).astype(BF16)
            parts.append(jnp.concatenate([hi, lo], axis=0))

        bd_ref[...] = jnp.dot(lm_ref[...], jnp.concatenate(parts, axis=1), preferred_element_type=F32)

    def mix(h, qk_ref, bd_ref):
        st = st_ref[h]
        for c in range(n_chunks):
            rows = pl.ds(c * chunk, chunk)
            cols = pl.ds(c * HEAD_DIM, HEAD_DIM)
            qq = qk_ref[0, rows, :]
            key = qk_ref[1, rows, :]
            val = i_ref[h, rows, :]

            lvl = lvl_ref[...]
            s0 = lax.dot_general(qq, key, _NT, preferred_element_type=F32)
            scores = jnp.where(lvl == n_levels, s0, 0.0)
            for k in range(n_levels):
                w = jnp.exp2(bd_ref[pl.ds(chunk * (k + 1), chunk), cols]).astype(BF16)
                sk = lax.dot_general(qq * w, key * w, _NT, preferred_element_type=F32)
                scores = jnp.where(lvl == k, sk, scores)

            b = bd_ref[pl.ds(0, chunk), cols]
            o = jnp.dot(scores.astype(BF16), val, preferred_element_type=F32)
            o = o + lax.dot_general(qq * jnp.exp2(b).astype(BF16), st.astype(BF16), _NT,
                                    preferred_element_type=F32)

            b_last = b[chunk - 1:chunk]
            k_out = key * jnp.exp2(b_last - b).astype(BF16)
            st = st * jnp.exp2(b_last) + lax.dot_general(val, k_out, _TN, preferred_element_type=F32)

            ms = jnp.mean(o * o, axis=-1, keepdims=True)
            on = o * lax.rsqrt(ms + EPS) * gain
            graw = g_ref[h, rows, :].astype(F32)
            o_ref[h, rows, :] = (on * _silu(graw)).astype(BF16)
        st_ref[h] = st

    gates(0, qk0_ref, bd0_ref)

    def pair_body(j, carry):
        h0 = 2 * j
        mix(h0, qk0_ref, bd0_ref)
        gates(h0 + 1, qk1_ref, bd1_ref)
        mix(h0 + 1, qk1_ref, bd1_ref)
        gates(jnp.minimum(h0 + 2, n_heads - 1), qk0_ref, bd0_ref)
        return carry

    lax.fori_loop(0, n_heads // 2, pair_body, 0)


def _hgrn(l, proj, lbs, norm_g, level_mat, batch, seq):
    n_heads = lbs.shape[0]
    t = proj.shape[1]
    tile = HGRN_TILE
    per_seq = seq // tile
    n_chunks = tile // HGRN_CHUNK
    proj5 = proj.reshape(5, n_heads, t, LANES)

    def sec_spec(sec):
        return pl.BlockSpec((None, n_heads, tile, LANES),
                            lambda b, s, l: (sec, 0, b * per_seq + s, 0))

    grid_spec = pltpu.PrefetchScalarGridSpec(
        num_scalar_prefetch=1,
        grid=(batch, per_seq),
        in_specs=[
            sec_spec(0), sec_spec(1), sec_spec(2), sec_spec(3),
            pl.BlockSpec(lbs.shape, lambda b, s, l: (0, 0, 0)),
            pl.BlockSpec((None, 1, HEAD_DIM), lambda b, s, l: (l[0], 0, 0)),
            pl.BlockSpec(level_mat.shape, lambda b, s, l: (0, 0)),
        ],
        out_specs=pl.BlockSpec((n_heads, tile, LANES), lambda b, s, l: (0, b * per_seq + s, 0)),
        scratch_shapes=[pltpu.VMEM((n_heads, HEAD_DIM, HEAD_DIM), F32),
                        pltpu.VMEM((HGRN_CHUNK, HGRN_CHUNK), jnp.int32),
                        pltpu.VMEM((2, tile, HEAD_DIM), BF16),
                        pltpu.VMEM((2, tile, HEAD_DIM), BF16),
                        pltpu.VMEM((level_mat.shape[0], n_chunks * HEAD_DIM), F32),
                        pltpu.VMEM((level_mat.shape[0], n_chunks * HEAD_DIM), F32)],
    )
    return pl.pallas_call(
        _hgrn_kernel,
        grid_spec=grid_spec,
        out_shape=jax.ShapeDtypeStruct((n_heads, t, LANES), BF16),
        compiler_params=_params(2),
        name="hgrn2",
    )(l, proj5, proj5, proj5, proj5, lbs, norm_g, level_mat)


def _pool_kernel(l_ref, u_ref, pw_ref, ps_ref, o_ref, ext_ref):
    grp = pl.program_id(1)
    s = pl.program_id(2)
    tile = u_ref.shape[1]
    width = jnp.left_shift(2, grp)

    @pl.when(s == 0)
    def _():
        ext_ref[0:POOL_HALO, :] = jnp.zeros((POOL_HALO, ext_ref.shape[1]), BF16)

    u = jnp.concatenate([u_ref[0], u_ref[1]], axis=1)
    ext_ref[POOL_HALO:, :] = u

    trow = lax.broadcasted_iota(jnp.int32, (tile, tile + POOL_HALO), 0) + POOL_HALO
    rcol = lax.broadcasted_iota(jnp.int32, (tile, tile + POOL_HALO), 1)
    band = ((rcol <= trow) & (rcol > trow - width)).astype(BF16)
    sums = jnp.dot(band, ext_ref[...], preferred_element_type=F32)

    pos = lax.broadcasted_iota(jnp.int32, (tile, 1), 0) + s * tile + 1
    count = jnp.minimum(pos, width).astype(F32)
    pooled = sums / count - u.astype(F32)
    y = jnp.dot(pooled.astype(BF16), pw_ref[...], preferred_element_type=F32)
    o_ref[...] = (y * ps_ref[...]).astype(BF16)
    ext_ref[0:POOL_HALO, :] = u[tile - POOL_HALO:, :]


def _pool(l, proj, pool_w, pool_scale, batch, seq):
    t = proj.shape[1]
    depth, groups, gd, _ = pool_w.shape
    tile = POOL_TILE
    per_seq = seq // tile
    blocks_per_group = gd // LANES
    first_block = proj.shape[0] - groups * blocks_per_group
    grid_spec = pltpu.PrefetchScalarGridSpec(
        num_scalar_prefetch=1,
        grid=(batch, groups, per_seq),
        in_specs=[
            pl.BlockSpec((blocks_per_group, tile, LANES),
                         lambda b, g, s, l: (first_block // blocks_per_group + g, b * per_seq + s, 0)),
            pl.BlockSpec((None, None, gd, gd), lambda b, g, s, l: (l[0], g, 0, 0)),
            pl.BlockSpec((None, None, 1, gd), lambda b, g, s, l: (l[0], g, 0, 0)),
        ],
        out_specs=pl.BlockSpec((tile, gd), lambda b, g, s, l: (b * per_seq + s, g)),
        scratch_shapes=[pltpu.VMEM((tile + POOL_HALO, gd), BF16)],
    )
    return pl.pallas_call(
        _pool_kernel,
        grid_spec=grid_spec,
        out_shape=jax.ShapeDtypeStruct((t, groups * gd), BF16),
        compiler_params=_params(3),
        name="pool_mix",
    )(l, proj, pool_w, pool_scale.reshape(depth, groups, 1, gd))


def _outproj_kernel(l_ref, ya_ref, yb_ref, w_ref, x_ref, gate_ref, o_ref):
    y = jnp.concatenate([ya_ref[h] for h in range(ya_ref.shape[0])] + [yb_ref[...]], axis=1)
    acc = jnp.dot(y, w_ref[...], preferred_element_type=F32)
    o_ref[...] = x_ref[...] + gate_ref[...] * acc


def _out_proj(l, ya, yb, w_out, x2d, mods, seq):
    t, d = x2d.shape
    n_heads = ya.shape[0]
    tm = OUTPROJ_TM
    per_seq = seq // tm
    grid_spec = pltpu.PrefetchScalarGridSpec(
        num_scalar_prefetch=1,
        grid=(t // tm,),
        in_specs=[
            pl.BlockSpec((n_heads, tm, LANES), lambda i, l: (0, i, 0)),
            pl.BlockSpec((tm, yb.shape[1]), lambda i, l: (i, 0)),
            pl.BlockSpec((None, d, d), lambda i, l: (l[0], 0, 0)),
            pl.BlockSpec((tm, d), lambda i, l: (i, 0)),
            pl.BlockSpec((None, None, None, 1, d), lambda i, l: (l[0], i // per_seq, 2, 0, 0)),
        ],
        out_specs=pl.BlockSpec((tm, d), lambda i, l: (i, 0)),
    )
    return pl.pallas_call(
        _outproj_kernel,
        grid_spec=grid_spec,
        out_shape=jax.ShapeDtypeStruct((t, d), F32),
        compiler_params=_params(1),
        name="out_proj",
    )(l, ya, yb, w_out, x2d, mods)


def _shift_rows(a, k, halo):
    rolled = pltpu.roll(a, k, axis=0)
    prev = pltpu.roll(halo, k, axis=0)
    ri = lax.broadcasted_iota(jnp.int32, halo.shape, 0)
    top = jnp.where(ri < k, prev, rolled[0:halo.shape[0]])
    return jnp.concatenate([top, rolled[halo.shape[0]:]], axis=0)


def _ffn_kernel(l_ref, x_ref, g_ref, sh_ref, sc_ref, gate_ref, wa_ref, wv_ref, cw_ref, cb_ref, wd_ref,
                o_ref, h_ref, acc_ref, halo_ref, *, tiles_per_seq):
    i = pl.program_id(0)
    j = pl.program_id(1)
    halo_rows = halo_ref.shape[1]

    @pl.when(j == 0)
    def _():
        h_ref[...] = _norm_mod(x_ref[...], g_ref[...], sc_ref[...], sh_ref[...]).astype(BF16)
        acc_ref[...] = jnp.zeros_like(acc_ref)

    @pl.when(i % tiles_per_seq == 0)
    def _():
        halo_ref[j] = jnp.zeros(halo_ref.shape[1:], F32)

    h = h_ref[...]
    a = jnp.dot(h, wa_ref[...], preferred_element_type=F32)
    v = jnp.dot(h, wv_ref[...], preferred_element_type=F32)
    halo = halo_ref[j]
    cw = cw_ref[...]
    conv = cb_ref[...] + cw[CONV_WIDTH - 1:CONV_WIDTH] * a
    for back in range(1, CONV_WIDTH):
        tap = CONV_WIDTH - 1 - back
        conv = conv + cw[tap:tap + 1] * _shift_rows(a, back, halo)
    halo_ref[j] = a[a.shape[0] - halo_rows:, :]
    glu = (_silu(conv) * v).astype(BF16)
    acc_ref[...] += jnp.dot(glu, wd_ref[...], preferred_element_type=F32)

    @pl.when(j == pl.num_programs(1) - 1)
    def _():
        o_ref[...] = x_ref[...] + gate_ref[...] * acc_ref[...]


def _ffn(l, x2d, gains, mods, w_up, conv_w, conv_b, w_down, seq):
    t, d = x2d.shape
    depth, ff, _ = w_down.shape
    tm, tf = FFN_TM, FFN_TF
    per_seq = seq // tm
    n_ff = ff // tf

    def mod_spec(k):
        return pl.BlockSpec((None, None, None, 1, d), lambda i, j, l: (l[0], i // per_seq, k, 0, 0))

    grid_spec = pltpu.PrefetchScalarGridSpec(
        num_scalar_prefetch=1,
        grid=(t // tm, n_ff),
        in_specs=[
            pl.BlockSpec((tm, d), lambda i, j, l: (i, 0)),
            pl.BlockSpec((None, 1, d), lambda i, j, l: (l[0], 0, 0)),
            mod_spec(3), mod_spec(4), mod_spec(5),
            pl.BlockSpec((None, d, tf), lambda i, j, l: (l[0], 0, j)),
            pl.BlockSpec((None, d, tf), lambda i, j, l: (l[0], 0, n_ff + j)),
            pl.BlockSpec((None, CONV_WIDTH, tf), lambda i, j, l: (l[0], 0, j)),
            pl.BlockSpec((None, 1, tf), lambda i, j, l: (l[0], 0, j)),
            pl.BlockSpec((None, tf, d), lambda i, j, l: (l[0], j, 0)),
        ],
        out_specs=pl.BlockSpec((tm, d), lambda i, j, l: (i, 0)),
        scratch_shapes=[
            pltpu.VMEM((tm, d), BF16),
            pltpu.VMEM((tm, d), F32),
            pltpu.VMEM((n_ff, 8, tf), F32),
        ],
    )
    return pl.pallas_call(
        functools.partial(_ffn_kernel, tiles_per_seq=per_seq),
        grid_spec=grid_spec,
        out_shape=jax.ShapeDtypeStruct((t, d), F32),
        compiler_params=_params(2),
        name="conv_glu_ffn",
    )(l, x2d, gains, mods, mods, mods, w_up, w_up, conv_w, conv_b.reshape(depth, 1, ff), w_down)


def _final_norm_kernel(x_ref, g_ref, o_ref):
    xf = x_ref[...]
    ms = jnp.mean(xf * xf, axis=-1, keepdims=True)
    o_ref[...] = xf * lax.rsqrt(ms + EPS) * g_ref[...]


def _final_norm(x2d, gain):
    t, d = x2d.shape
    return pl.pallas_call(
        _final_norm_kernel,
        grid=(t // NORM_TM,),
        in_specs=[pl.BlockSpec((NORM_TM, d), lambda i: (i, 0)),
                  pl.BlockSpec((1, d), lambda i: (0, 0))],
        out_specs=pl.BlockSpec((NORM_TM, d), lambda i: (i, 0)),
        out_shape=jax.ShapeDtypeStruct((t, d), F32),
        compiler_params=_params(1),
        name="final_norm",
    )(x2d, gain.reshape(1, d))


def kernel(x, c, ada_w, ada_b, mix_norm_g, w_in, hgrn_lower_bounds, hgrn_norm_g, pool_w, pool_scale,
           w_out, ffn_norm_g, w_up, conv_w, conv_b, w_down, final_norm_g):
    batch, seq, d = x.shape
    depth = w_in.shape[0]
    d_hgrn = hgrn_lower_bounds.shape[1]
    n_heads = d_hgrn // HEAD_DIM

    mods = _modulation(c, ada_w, ada_b).reshape(depth, batch, N_MOD, 1, d)
    w_in_b, w_out_b = w_in.astype(BF16), w_out.astype(BF16)
    w_up_b, w_down_b, pool_w_b = w_up.astype(BF16), w_down.astype(BF16), pool_w.astype(BF16)
    lbs = hgrn_lower_bounds.reshape(depth, n_heads, HEAD_DIM).transpose(1, 0, 2)
    mix_g = mix_norm_g.reshape(depth, 1, d)
    ffn_g = ffn_norm_g.reshape(depth, 1, d)
    head_g = hgrn_norm_g.reshape(depth, 1, HEAD_DIM)
    level_mat = jnp.asarray(_hgrn_level_matrix(HGRN_CHUNK), dtype=BF16)

    xc = x.reshape(batch * seq, d)
    for li in range(depth):
        l = jnp.full((1,), li, jnp.int32)
        proj = _in_proj(l, xc, mix_g, mods, w_in_b, seq)
        ya = _hgrn(l, proj, lbs, head_g, level_mat, batch, seq)
        yb = _pool(l, proj, pool_w_b, pool_scale, batch, seq)
        x1 = _out_proj(l, ya, yb, w_out_b, xc, mods, seq)
        xc = _ffn(l, x1, ffn_g, mods, w_up_b, conv_w, conv_b, w_down_b, seq)
    return _final_norm(xc, final_norm_g).reshape(batch, seq, d)
```

```python
import functools

import jax
import jax.numpy as jnp
import numpy as np
from jax import lax
from jax.experimental import pallas as pl
from jax.experimental.pallas import tpu as pltpu

F32 = jnp.float32
BF16 = jnp.bfloat16

LANES = 128
HEAD_DIM = 128
POOL_WINDOWS = (2, 4, 8, 16)
POOL_HALO = 128
CONV_WIDTH = 3
N_MOD = 6
EPS = 1e-6
LOG2E = 1.4426950408889634
VMEM_LIMIT = 56 * 1024 * 1024

HGRN_CHUNK = 128
HGRN_TILE = 512
POOL_TILE = 1024
INPROJ_TM, INPROJ_TN = 1024, 1024
OUTPROJ_TM = 512
FFN_TM, FFN_TF = 512, 512
MOD_TN = 1024


def _params(n_axes, vmem=VMEM_LIMIT):
    return pltpu.CompilerParams(
        dimension_semantics=("arbitrary",) * n_axes, vmem_limit_bytes=vmem)


def _norm_mod(xf, gain, scale, shift):
    ms = jnp.mean(xf * xf, axis=-1, keepdims=True)
    y = xf * lax.rsqrt(ms + EPS) * gain
    return y * (1.0 + scale) + shift


def _silu(x):
    return (0.5 * x) * (1.0 + jnp.tanh(0.5 * x))


def _mod_kernel(c_ref, w_ref, b_ref, o_ref):
    c = c_ref[...]
    c_act = _silu(c).astype(BF16)
    acc = jnp.dot(c_act, w_ref[...].astype(BF16), preferred_element_type=F32)
    o_ref[...] = acc + b_ref[...]


def _modulation(c, ada_w, ada_b):
    depth, d, n = ada_w.shape
    b = c.shape[0]
    return pl.pallas_call(
        _mod_kernel,
        grid=(depth, n // MOD_TN),
        in_specs=[
            pl.BlockSpec((b, d), lambda l, j: (0, 0)),
            pl.BlockSpec((None, d, MOD_TN), lambda l, j: (l, 0, j)),
            pl.BlockSpec((None, 1, MOD_TN), lambda l, j: (l, 0, j)),
        ],
        out_specs=pl.BlockSpec((None, b, MOD_TN), lambda l, j: (l, 0, j)),
        out_shape=jax.ShapeDtypeStruct((depth, b, n), F32),
        compiler_params=_params(2),
        name="ada_mod",
    )(c, ada_w, ada_b.reshape(depth, 1, n))


def _inproj_kernel(l_ref, x_ref, g_ref, sh_ref, sc_ref, w_ref, o_ref, h_ref):
    @pl.when(pl.program_id(1) == 0)
    def _():
        h_ref[...] = _norm_mod(x_ref[...], g_ref[...], sc_ref[...], sh_ref[...]).astype(BF16)

    acc = jnp.dot(h_ref[...], w_ref[...], preferred_element_type=F32)
    for blk in range(o_ref.shape[0]):
        o_ref[blk] = acc[:, blk * LANES:(blk + 1) * LANES].astype(BF16)


def _in_proj(l, x2d, gains, mods, w_in, seq):
    t, d = x2d.shape
    n = w_in.shape[2]
    tm, tn = INPROJ_TM, INPROJ_TN
    per_seq = seq // tm
    grid_spec = pltpu.PrefetchScalarGridSpec(
        num_scalar_prefetch=1,
        grid=(t // tm, n // tn),
        in_specs=[
            pl.BlockSpec((tm, d), lambda i, j, l: (i, 0)),
            pl.BlockSpec((None, 1, d), lambda i, j, l: (l[0], 0, 0)),
            pl.BlockSpec((None, None, None, 1, d), lambda i, j, l: (l[0], i // per_seq, 0, 0, 0)),
            pl.BlockSpec((None, None, None, 1, d), lambda i, j, l: (l[0], i // per_seq, 1, 0, 0)),
            pl.BlockSpec((None, d, tn), lambda i, j, l: (l[0], 0, j)),
        ],
        out_specs=pl.BlockSpec((tn // LANES, tm, LANES), lambda i, j, l: (j, i, 0)),
        scratch_shapes=[pltpu.VMEM((tm, d), BF16)],
    )
    return pl.pallas_call(
        _inproj_kernel,
        grid_spec=grid_spec,
        out_shape=jax.ShapeDtypeStruct((n // LANES, t, LANES), BF16),
        compiler_params=_params(2),
        name="in_proj",
    )(l, x2d, gains, mods, mods, w_in)


def _hgrn_level_matrix(chunk):
    t = np.arange(chunk)[:, None]
    r = np.arange(chunk)[None, :]
    low = (r <= t).astype(np.float32)
    mats = [low]
    m = 1
    while m < chunk:
        anchor = (t // (2 * m)) * (2 * m) + m - 1
        sign = np.where(t > anchor, 1.0, -1.0)
        mats.append(sign * (low - (r <= anchor).astype(np.float32)))
        m *= 2
    stacked = np.concatenate(mats, axis=0)
    return np.concatenate([stacked] * 2, axis=1)


_NT = (((1,), (1,)), ((), ()))
_TN = (((0,), (0,)), ((), ()))


def _hgrn_kernel(l_ref, q_ref, f_ref, i_ref, g_ref, lb_ref, ng_ref, lm_ref, o_ref,
                 st_ref, lvl_ref, qk0_ref, qk1_ref, bd0_ref, bd1_ref):
    chunk = HGRN_CHUNK
    n_levels = chunk.bit_length() - 1
    n_heads, tile, _ = q_ref.shape
    n_chunks = tile // chunk
    layer = l_ref[0]

    @pl.when((pl.program_id(0) == 0) & (pl.program_id(1) == 0))
    def _():
        trow = lax.broadcasted_iota(jnp.int32, (chunk, chunk), 0)
        tcol = lax.broadcasted_iota(jnp.int32, (chunk, chunk), 1)
        txor = trow ^ tcol
        top = jnp.full((chunk, chunk), -1, jnp.int32)
        for k in range(n_levels):
            top = top + ((txor >> k) != 0).astype(jnp.int32)
        lvl_ref[...] = jnp.where(trow > tcol, top, jnp.where(trow == tcol, n_levels, -1))

    @pl.when(pl.program_id(1) == 0)
    def _():
        st_ref[...] = jnp.zeros_like(st_ref)

    gain = ng_ref[...]

    def gates(h, qk_ref, bd_ref):
        lbraw = lb_ref[h]
        ex = jnp.exp(lbraw - jnp.max(lbraw, axis=0, keepdims=True))
        p = ex / jnp.sum(ex, axis=0, keepdims=True)
        lrow = lax.broadcasted_iota(jnp.int32, p.shape, 0)
        cum = jnp.sum(jnp.where(lrow <= layer, p, 0.0), axis=0, keepdims=True)
        lb = jnp.clip(cum - p[0:1], 0.0, 1.0)
        half_key = 0.5 * (1.0 - lb)

        parts = []
        for c in range(n_chunks):
            rows = pl.ds(c * chunk, chunk)
            z = f_ref[h, rows, :].astype(F32)
            e = jnp.exp2(z * (-LOG2E))
            logf = jnp.log2(1.0 + lb * e) - jnp.log2(1.0 + e)
            key = half_key - half_key * jnp.tanh(0.5 * z)
            qraw = q_ref[h, rows, :].astype(F32)
            qq = (qraw * (0.5 * HEAD_DIM ** -0.5)) * (1.0 + jnp.tanh(0.5 * qraw))
            qk_ref[0, rows, :] = qq.astype(BF16)
            qk_ref[1, rows, :] = key.astype(BF16)
            hi = logf.astype(BF16)
            lo = (logf - hi.astype(F32)).astype(BF16)
            parts.append(jnp.concatenate([hi, lo], axis=0))

        bd_ref[...] = jnp.dot(lm_ref[...], jnp.concatenate(parts, axis=1), preferred_element_type=F32)

    def mix(h, qk_ref, bd_ref):
        st = st_ref[h]
        for c in range(n_chunks):
            rows = pl.ds(c * chunk, chunk)
            cols = pl.ds(c * HEAD_DIM, HEAD_DIM)
            qq = qk_ref[0, rows, :]
            key = qk_ref[1, rows, :]
            val = i_ref[h, rows, :]

            lvl = lvl_ref[...]
            s0 = lax.dot_general(qq, key, _NT, preferred_element_type=F32)
            scores = jnp.where(lvl == n_levels, s0, 0.0)
            for k in range(n_levels):
                w = jnp.exp2(bd_ref[pl.ds(chunk * (k + 1), chunk), cols]).astype(BF16)
                sk = lax.dot_general(qq * w, key * w, _NT, preferred_element_type=F32)
                scores = jnp.where(lvl == k, sk, scores)

            b = bd_ref[pl.ds(0, chunk), cols]
            o = jnp.dot(scores.astype(BF16), val, preferred_element_type=F32)
            o = o + lax.dot_general(qq * jnp.exp2(b).astype(BF16), st.astype(BF16), _NT,
                                    preferred_element_type=F32)

            b_last = b[chunk - 1:chunk]
            k_out = key * jnp.exp2(b_last - b).astype(BF16)
            st = st * jnp.exp2(b_last) + lax.dot_general(val, k_out, _TN, preferred_element_type=F32)

            ms = jnp.mean(o * o, axis=-1, keepdims=True)
            on = o * lax.rsqrt(ms + EPS) * gain
            graw = g_ref[h, rows, :].astype(F32)
            o_ref[h, rows, :] = (on * _silu(graw)).astype(BF16)
        st_ref[h] = st

    gates(0, qk0_ref, bd0_ref)

    def pair_body(j, carry):
        h0 = 2 * j
        mix(h0, qk0_ref, bd0_ref)
        gates(h0 + 1, qk1_ref, bd1_ref)
        mix(h0 + 1, qk1_ref, bd1_ref)
        gates(jnp.minimum(h0 + 2, n_heads - 1), qk0_ref, bd0_ref)
        return carry

    lax.fori_loop(0, n_heads // 2, pair_body, 0)


def _hgrn(l, proj, lbs, norm_g, level_mat, batch, seq):
    n_heads = lbs.shape[0]
    t = proj.shape[1]
    tile = HGRN_TILE
    per_seq = seq // tile
    n_chunks = tile // HGRN_CHUNK
    proj5 = proj.reshape(5, n_heads, t, LANES)

    def sec_spec(sec):
        return pl.BlockSpec((None, n_heads, tile, LANES),
                            lambda b, s, l: (sec, 0, b * per_seq + s, 0))

    grid_spec = pltpu.PrefetchScalarGridSpec(
        num_scalar_prefetch=1,
        grid=(batch, per_seq),
        in_specs=[
            sec_spec(0), sec_spec(1), sec_spec(2), sec_spec(3),
            pl.BlockSpec(lbs.shape, lambda b, s, l: (0, 0, 0)),
            pl.BlockSpec((None, 1, HEAD_DIM), lambda b, s, l: (l[0], 0, 0)),
            pl.BlockSpec(level_mat.shape, lambda b, s, l: (0, 0)),
        ],
        out_specs=pl.BlockSpec((n_heads, tile, LANES), lambda b, s, l: (0, b * per_seq + s, 0)),
        scratch_shapes=[pltpu.VMEM((n_heads, HEAD_DIM, HEAD_DIM), F32),
                        pltpu.VMEM((HGRN_CHUNK, HGRN_CHUNK), jnp.int32),
                        pltpu.VMEM((2, tile, HEAD_DIM), BF16),
                        pltpu.VMEM((2, tile, HEAD_DIM), BF16),
                        pltpu.VMEM((level_mat.shape[0], n_chunks * HEAD_DIM), F32),
                        pltpu.VMEM((level_mat.shape[0], n_chunks * HEAD_DIM), F32)],
    )
    return pl.pallas_call(
        _hgrn_kernel,
        grid_spec=grid_spec,
        out_shape=jax.ShapeDtypeStruct((n_heads, t, LANES), BF16),
        compiler_params=_params(2),
        name="hgrn2",
    )(l, proj5, proj5, proj5, proj5, lbs, norm_g, level_mat)


def _pool_kernel(l_ref, u_ref, pw_ref, ps_ref, o_ref, halo_ref):
    grp = pl.program_id(1)
    s = pl.program_id(2)
    tile = u_ref.shape[1]
    width = jnp.left_shift(2, grp)

    @pl.when(s == 0)
    def _():
        halo_ref[...] = jnp.zeros_like(halo_ref)

    u = jnp.concatenate([u_ref[0], u_ref[1]], axis=1)
    ext = jnp.concatenate([halo_ref[...], u], axis=0)

    trow = lax.broadcasted_iota(jnp.int32, (POOL_HALO, 2 * POOL_HALO), 0) + POOL_HALO
    rcol = lax.broadcasted_iota(jnp.int32, (POOL_HALO, 2 * POOL_HALO), 1)
    band = ((rcol <= trow) & (rcol > trow - width)).astype(BF16)
    sums = jnp.concatenate(
        [jnp.dot(band, ext[r * POOL_HALO:(r + 2) * POOL_HALO], preferred_element_type=F32)
         for r in range(tile // POOL_HALO)], axis=0)

    pos = lax.broadcasted_iota(jnp.int32, (tile, 1), 0) + s * tile + 1
    count = jnp.minimum(pos, width).astype(F32)
    pooled = sums / count - u.astype(F32)
    y = jnp.dot(pooled.astype(BF16), pw_ref[...], preferred_element_type=F32)
    o_ref[...] = (y * ps_ref[...]).astype(BF16)
    halo_ref[...] = u[tile - POOL_HALO:, :]


def _pool(l, proj, pool_w, pool_scale, batch, seq):
    t = proj.shape[1]
    depth, groups, gd, _ = pool_w.shape
    tile = POOL_TILE
    per_seq = seq // tile
    blocks_per_group = gd // LANES
    first_block = proj.shape[0] - groups * blocks_per_group
    grid_spec = pltpu.PrefetchScalarGridSpec(
        num_scalar_prefetch=1,
        grid=(batch, groups, per_seq),
        in_specs=[
            pl.BlockSpec((blocks_per_group, tile, LANES),
                         lambda b, g, s, l: (first_block // blocks_per_group + g, b * per_seq + s, 0)),
            pl.BlockSpec((None, None, gd, gd), lambda b, g, s, l: (l[0], g, 0, 0)),
            pl.BlockSpec((None, None, 1, gd), lambda b, g, s, l: (l[0], g, 0, 0)),
        ],
        out_specs=pl.BlockSpec((tile, gd), lambda b, g, s, l: (b * per_seq + s, g)),
        scratch_shapes=[pltpu.VMEM((POOL_HALO, gd), BF16)],
    )
    return pl.pallas_call(
        _pool_kernel,
        grid_spec=grid_spec,
        out_shape=jax.ShapeDtypeStruct((t, groups * gd), BF16),
        compiler_params=_params(3),
        name="pool_mix",
    )(l, proj, pool_w, pool_scale.reshape(depth, groups, 1, gd))


def _outproj_kernel(l_ref, ya_ref, yb_ref, w_ref, x_ref, gate_ref, g_ref, sh_ref, sc_ref, o_ref, h_ref):
    y = jnp.concatenate([ya_ref[h] for h in range(ya_ref.shape[0])] + [yb_ref[...]], axis=1)
    acc = jnp.dot(y, w_ref[...], preferred_element_type=F32)
    x1 = x_ref[...] + gate_ref[...] * acc
    o_ref[...] = x1
    h_ref[...] = _norm_mod(x1, g_ref[...], sc_ref[...], sh_ref[...]).astype(BF16)


def _out_proj(l, ya, yb, w_out, x2d, gains, mods, seq):
    t, d = x2d.shape
    n_heads = ya.shape[0]
    tm = OUTPROJ_TM
    per_seq = seq // tm

    def mod_spec(k):
        return pl.BlockSpec((None, None, None, 1, d), lambda i, l: (l[0], i // per_seq, k, 0, 0))

    grid_spec = pltpu.PrefetchScalarGridSpec(
        num_scalar_prefetch=1,
        grid=(t // tm,),
        in_specs=[
            pl.BlockSpec((n_heads, tm, LANES), lambda i, l: (0, i, 0)),
            pl.BlockSpec((tm, yb.shape[1]), lambda i, l: (i, 0)),
            pl.BlockSpec((None, d, d), lambda i, l: (l[0], 0, 0)),
            pl.BlockSpec((tm, d), lambda i, l: (i, 0)),
            mod_spec(2),
            pl.BlockSpec((None, 1, d), lambda i, l: (l[0], 0, 0)),
            mod_spec(3), mod_spec(4),
        ],
        out_specs=[pl.BlockSpec((tm, d), lambda i, l: (i, 0)),
                   pl.BlockSpec((tm, d), lambda i, l: (i, 0))],
    )
    return pl.pallas_call(
        _outproj_kernel,
        grid_spec=grid_spec,
        out_shape=[jax.ShapeDtypeStruct((t, d), F32), jax.ShapeDtypeStruct((t, d), BF16)],
        compiler_params=_params(1),
        name="out_proj",
    )(l, ya, yb, w_out, x2d, mods, gains, mods, mods)


def _shift_rows(a, k, halo):
    rolled = pltpu.roll(a, k, axis=0)
    prev = pltpu.roll(halo, k, axis=0)
    ri = lax.broadcasted_iota(jnp.int32, halo.shape, 0)
    top = jnp.where(ri < k, prev, rolled[0:halo.shape[0]])
    return jnp.concatenate([top, rolled[halo.shape[0]:]], axis=0)


def _ffn_kernel(l_ref, x_ref, h_ref, gate_ref, fg_ref, wa_ref, wv_ref, cw_ref, cb_ref, wd_ref,
                o_ref, acc_ref, halo_ref, *, tiles_per_seq, depth):
    i = pl.program_id(0)
    j = pl.program_id(1)
    halo_rows = halo_ref.shape[1]

    @pl.when(j == 0)
    def _():
        acc_ref[...] = jnp.zeros_like(acc_ref)

    @pl.when(i % tiles_per_seq == 0)
    def _():
        halo_ref[j] = jnp.zeros(halo_ref.shape[1:], F32)

    h = h_ref[...]
    a = jnp.dot(h, wa_ref[...], preferred_element_type=F32)
    v = jnp.dot(h, wv_ref[...], preferred_element_type=F32)
    halo = halo_ref[j]
    cw = cw_ref[...]
    conv = cb_ref[...] + cw[CONV_WIDTH - 1:CONV_WIDTH] * a
    for back in range(1, CONV_WIDTH):
        tap = CONV_WIDTH - 1 - back
        conv = conv + cw[tap:tap + 1] * _shift_rows(a, back, halo)
    halo_ref[j] = a[a.shape[0] - halo_rows:, :]
    glu = (_silu(conv) * v).astype(BF16)
    acc_ref[...] += jnp.dot(glu, wd_ref[...], preferred_element_type=F32)

    last_step = j == pl.num_programs(1) - 1
    last_layer = l_ref[0] == depth - 1

    @pl.when(last_step & jnp.logical_not(last_layer))
    def _():
        o_ref[...] = x_ref[...] + gate_ref[...] * acc_ref[...]

    @pl.when(last_step & last_layer)
    def _():
        xf = x_ref[...] + gate_ref[...] * acc_ref[...]
        ms = jnp.mean(xf * xf, axis=-1, keepdims=True)
        o_ref[...] = xf * lax.rsqrt(ms + EPS) * fg_ref[...]


def _ffn(l, x2d, h2d, mods, final_g, w_up, conv_w, conv_b, w_down, seq):
    t, d = x2d.shape
    depth, ff, _ = w_down.shape
    tm, tf = FFN_TM, FFN_TF
    per_seq = seq // tm
    n_ff = ff // tf

    grid_spec = pltpu.PrefetchScalarGridSpec(
        num_scalar_prefetch=1,
        grid=(t // tm, n_ff),
        in_specs=[
            pl.BlockSpec((tm, d), lambda i, j, l: (i, 0)),
            pl.BlockSpec((tm, d), lambda i, j, l: (i, 0)),
            pl.BlockSpec((None, None, None, 1, d), lambda i, j, l: (l[0], i // per_seq, 5, 0, 0)),
            pl.BlockSpec((1, d), lambda i, j, l: (0, 0)),
            pl.BlockSpec((None, d, tf), lambda i, j, l: (l[0], 0, j)),
            pl.BlockSpec((None, d, tf), lambda i, j, l: (l[0], 0, n_ff + j)),
            pl.BlockSpec((None, CONV_WIDTH, tf), lambda i, j, l: (l[0], 0, j)),
            pl.BlockSpec((None, 1, tf), lambda i, j, l: (l[0], 0, j)),
            pl.BlockSpec((None, tf, d), lambda i, j, l: (l[0], j, 0)),
        ],
        out_specs=pl.BlockSpec((tm, d), lambda i, j, l: (i, 0)),
        scratch_shapes=[
            pltpu.VMEM((tm, d), F32),
            pltpu.VMEM((n_ff, 8, tf), F32),
        ],
    )
    return pl.pallas_call(
        functools.partial(_ffn_kernel, tiles_per_seq=per_seq, depth=depth),
        grid_spec=grid_spec,
        out_shape=jax.ShapeDtypeStruct((t, d), F32),
        compiler_params=_params(2),
        name="conv_glu_ffn",
    )(l, x2d, h2d, mods, final_g, w_up, w_up, conv_w, conv_b.reshape(depth, 1, ff), w_down)


def kernel(x, c, ada_w, ada_b, mix_norm_g, w_in, hgrn_lower_bounds, hgrn_norm_g, pool_w, pool_scale,
           w_out, ffn_norm_g, w_up, conv_w, conv_b, w_down, final_norm_g):
    batch, seq, d = x.shape
    depth = w_in.shape[0]
    d_hgrn = hgrn_lower_bounds.shape[1]
    n_heads = d_hgrn // HEAD_DIM

    mods = _modulation(c, ada_w, ada_b).reshape(depth, batch, N_MOD, 1, d)
    w_in_b, w_out_b = w_in.astype(BF16), w_out.astype(BF16)
    w_up_b, w_down_b, pool_w_b = w_up.astype(BF16), w_down.astype(BF16), pool_w.astype(BF16)
    lbs = hgrn_lower_bounds.reshape(depth, n_heads, HEAD_DIM).transpose(1, 0, 2)
    mix_g = mix_norm_g.reshape(depth, 1, d)
    ffn_g = ffn_norm_g.reshape(depth, 1, d)
    head_g = hgrn_norm_g.reshape(depth, 1, HEAD_DIM)
    final_g = final_norm_g.reshape(1, d)
    level_mat = jnp.asarray(_hgrn_level_matrix(HGRN_CHUNK), dtype=BF16)

    xc = x.reshape(batch * seq, d)
    for li in range(depth):
        l = jnp.full((1,), li, jnp.int32)
        proj = _in_proj(l, xc, mix_g, mods, w_in_b, seq)
        ya = _hgrn(l, proj, lbs, head_g, level_mat, batch, seq)
        yb = _pool(l, proj, pool_w_b, pool_scale, batch, seq)
        x1, h2 = _out_proj(l, ya, yb, w_out_b, xc, ffn_g, mods, seq)
        xc = _ffn(l, x1, h2, mods, final_g, w_up_b, conv_w, conv_b, w_down_b, seq)
    return xc.reshape(batch, seq, d)
```

```python
import functools

import jax
import jax.numpy as jnp
import numpy as np
from jax import lax
from jax.experimental import pallas as pl
from jax.experimental.pallas import tpu as pltpu

F32 = jnp.float32
BF16 = jnp.bfloat16

LANES = 128
HEAD_DIM = 128
POOL_WINDOWS = (2, 4, 8, 16)
POOL_HALO = 128
CONV_WIDTH = 3
N_MOD = 6
EPS = 1e-6
LOG2E = 1.4426950408889634
VMEM_LIMIT = 56 * 1024 * 1024

HGRN_CHUNK = 128
HGRN_TILE = 1024
POOL_TILE = 1024
INPROJ_TM, INPROJ_TN = 1024, 1024
OUTPROJ_TM = 512
FFN_TM, FFN_TF = 512, 512
MOD_TN = 1024


def _params(n_axes, vmem=VMEM_LIMIT):
    return pltpu.CompilerParams(
        dimension_semantics=("arbitrary",) * n_axes, vmem_limit_bytes=vmem)


def _norm_mod(xf, gain, scale, shift):
    ms = jnp.mean(xf * xf, axis=-1, keepdims=True)
    y = xf * lax.rsqrt(ms + EPS) * gain
    return y * (1.0 + scale) + shift


def _silu(x):
    return (0.5 * x) * (1.0 + jnp.tanh(0.5 * x))


def _mod_kernel(c_ref, w_ref, b_ref, o_ref):
    c = c_ref[...]
    c_act = _silu(c).astype(BF16)
    acc = jnp.dot(c_act, w_ref[...].astype(BF16), preferred_element_type=F32)
    o_ref[...] = acc + b_ref[...]


def _modulation(c, ada_w, ada_b):
    depth, d, n = ada_w.shape
    b = c.shape[0]
    return pl.pallas_call(
        _mod_kernel,
        grid=(depth, n // MOD_TN),
        in_specs=[
            pl.BlockSpec((b, d), lambda l, j: (0, 0)),
            pl.BlockSpec((None, d, MOD_TN), lambda l, j: (l, 0, j)),
            pl.BlockSpec((None, 1, MOD_TN), lambda l, j: (l, 0, j)),
        ],
        out_specs=pl.BlockSpec((None, b, MOD_TN), lambda l, j: (l, 0, j)),
        out_shape=jax.ShapeDtypeStruct((depth, b, n), F32),
        compiler_params=_params(2),
        name="ada_mod",
    )(c, ada_w, ada_b.reshape(depth, 1, n))


def _inproj_kernel(l_ref, x_ref, g_ref, sh_ref, sc_ref, w_ref, o_ref, h_ref):
    @pl.when(pl.program_id(1) == 0)
    def _():
        h_ref[...] = _norm_mod(x_ref[...], g_ref[...], sc_ref[...], sh_ref[...]).astype(BF16)

    acc = jnp.dot(h_ref[...], w_ref[...], preferred_element_type=F32)
    for blk in range(o_ref.shape[0]):
        o_ref[blk] = acc[:, blk * LANES:(blk + 1) * LANES].astype(BF16)


def _in_proj(l, x2d, gains, mods, w_in, seq):
    t, d = x2d.shape
    n = w_in.shape[2]
    tm, tn = INPROJ_TM, INPROJ_TN
    per_seq = seq // tm
    grid_spec = pltpu.PrefetchScalarGridSpec(
        num_scalar_prefetch=1,
        grid=(t // tm, n // tn),
        in_specs=[
            pl.BlockSpec((tm, d), lambda i, j, l: (i, 0)),
            pl.BlockSpec((None, 1, d), lambda i, j, l: (l[0], 0, 0)),
            pl.BlockSpec((None, None, None, 1, d), lambda i, j, l: (l[0], i // per_seq, 0, 0, 0)),
            pl.BlockSpec((None, None, None, 1, d), lambda i, j, l: (l[0], i // per_seq, 1, 0, 0)),
            pl.BlockSpec((None, d, tn), lambda i, j, l: (l[0], 0, j)),
        ],
        out_specs=pl.BlockSpec((tn // LANES, tm, LANES), lambda i, j, l: (j, i, 0)),
        scratch_shapes=[pltpu.VMEM((tm, d), BF16)],
    )
    return pl.pallas_call(
        _inproj_kernel,
        grid_spec=grid_spec,
        out_shape=jax.ShapeDtypeStruct((n // LANES, t, LANES), BF16),
        compiler_params=_params(2),
        name="in_proj",
    )(l, x2d, gains, mods, mods, w_in)


def _hgrn_level_matrix(chunk):
    t = np.arange(chunk)[:, None]
    r = np.arange(chunk)[None, :]
    low = (r <= t).astype(np.float32)
    mats = [low]
    m = 1
    while m < chunk:
        anchor = (t // (2 * m)) * (2 * m) + m - 1
        sign = np.where(t > anchor, 1.0, -1.0)
        mats.append(sign * (low - (r <= anchor).astype(np.float32)))
        m *= 2
    stacked = np.concatenate(mats, axis=0)
    return np.concatenate([stacked] * 2, axis=1)


_NT = (((1,), (1,)), ((), ()))
_TN = (((0,), (0,)), ((), ()))


def _hgrn_kernel(l_ref, q_ref, f_ref, i_ref, g_ref, lb_ref, ng_ref, lm_ref, o_ref,
                 st_ref, lvl_ref, qk0_ref, qk1_ref, bd0_ref, bd1_ref):
    chunk = HGRN_CHUNK
    n_levels = chunk.bit_length() - 1
    n_heads, tile, _ = q_ref.shape
    n_chunks = tile // chunk
    layer = l_ref[0]

    @pl.when((pl.program_id(0) == 0) & (pl.program_id(1) == 0))
    def _():
        trow = lax.broadcasted_iota(jnp.int32, (chunk, chunk), 0)
        tcol = lax.broadcasted_iota(jnp.int32, (chunk, chunk), 1)
        txor = trow ^ tcol
        top = jnp.full((chunk, chunk), -1, jnp.int32)
        for k in range(n_levels):
            top = top + ((txor >> k) != 0).astype(jnp.int32)
        lvl_ref[...] = jnp.where(trow > tcol, top, jnp.where(trow == tcol, n_levels, -1))

    @pl.when(pl.program_id(1) == 0)
    def _():
        st_ref[...] = jnp.zeros_like(st_ref)

    gain = ng_ref[...]

    def gates(h, qk_ref, bd_ref):
        lbraw = lb_ref[h]
        ex = jnp.exp(lbraw - jnp.max(lbraw, axis=0, keepdims=True))
        p = ex / jnp.sum(ex, axis=0, keepdims=True)
        lrow = lax.broadcasted_iota(jnp.int32, p.shape, 0)
        cum = jnp.sum(jnp.where(lrow <= layer, p, 0.0), axis=0, keepdims=True)
        lb = jnp.clip(cum - p[0:1], 0.0, 1.0)
        half_key = 0.5 * (1.0 - lb)

        parts = []
        for c in range(n_chunks):
            rows = pl.ds(c * chunk, chunk)
            z = f_ref[h, rows, :].astype(F32)
            e = jnp.exp2(z * (-LOG2E))
            logf = (jnp.log(1.0 + lb * e) - jnp.log(1.0 + e)) * LOG2E
            key = half_key - half_key * jnp.tanh(0.5 * z)
            qraw = q_ref[h, rows, :].astype(F32)
            qq = (qraw * (0.5 * HEAD_DIM ** -0.5)) * (1.0 + jnp.tanh(0.5 * qraw))
            qk_ref[0, rows, :] = qq.astype(BF16)
            qk_ref[1, rows, :] = key.astype(BF16)
            hi = logf.astype(BF16)
            lo = (logf - hi.astype(F32)).astype(BF16)
            parts.append(jnp.concatenate([hi, lo], axis=0))

        bd_ref[...] = jnp.dot(lm_ref[...], jnp.concatenate(parts, axis=1), preferred_element_type=F32)

    def mix(h, qk_ref, bd_ref):
        st = st_ref[h]
        lvl = lvl_ref[...]
        all_scores = []
        for c in range(n_chunks):
            rows = pl.ds(c * chunk, chunk)
            s0 = lax.dot_general(qk_ref[0, rows, :], qk_ref[1, rows, :], _NT, preferred_element_type=F32)
            all_scores.append(jnp.where(lvl == n_levels, s0, 0.0))
        for k in range(n_levels):
            for c in range(n_chunks):
                rows = pl.ds(c * chunk, chunk)
                cols = pl.ds(c * HEAD_DIM, HEAD_DIM)
                w = jnp.exp2(bd_ref[pl.ds(chunk * (k + 1), chunk), cols]).astype(BF16)
                sk = lax.dot_general(qk_ref[0, rows, :] * w, qk_ref[1, rows, :] * w, _NT,
                                     preferred_element_type=F32)
                all_scores[c] = jnp.where(lvl == k, sk, all_scores[c])
        all_scores = [s.astype(BF16) for s in all_scores]

        for c in range(n_chunks):
            rows = pl.ds(c * chunk, chunk)
            cols = pl.ds(c * HEAD_DIM, HEAD_DIM)
            qq = qk_ref[0, rows, :]
            key = qk_ref[1, rows, :]
            val = i_ref[h, rows, :]
            b = bd_ref[pl.ds(0, chunk), cols]
            o = jnp.dot(all_scores[c], val, preferred_element_type=F32)
            o = o + lax.dot_general(qq * jnp.exp2(b).astype(BF16), st.astype(BF16), _NT,
                                    preferred_element_type=F32)

            b_last = b[chunk - 1:chunk]
            k_out = key * jnp.exp2(b_last - b).astype(BF16)
            st = st * jnp.exp2(b_last) + lax.dot_general(val, k_out, _TN, preferred_element_type=F32)

            ms = jnp.mean(o * o, axis=-1, keepdims=True)
            on = o * lax.rsqrt(ms + EPS) * gain
            graw = g_ref[h, rows, :].astype(F32)
            o_ref[h, rows, :] = (on * _silu(graw)).astype(BF16)
        st_ref[h] = st

    gates(0, qk0_ref, bd0_ref)

    def pair_body(j, carry):
        h0 = 2 * j
        mix(h0, qk0_ref, bd0_ref)
        gates(h0 + 1, qk1_ref, bd1_ref)
        mix(h0 + 1, qk1_ref, bd1_ref)
        gates(jnp.minimum(h0 + 2, n_heads - 1), qk0_ref, bd0_ref)
        return carry

    lax.fori_loop(0, n_heads // 2, pair_body, 0)


def _hgrn(l, proj, lbs, norm_g, level_mat, batch, seq):
    n_heads = lbs.shape[0]
    t = proj.shape[1]
    tile = HGRN_TILE
    per_seq = seq // tile
    n_chunks = tile // HGRN_CHUNK
    proj5 = proj.reshape(5, n_heads, t, LANES)

    def sec_spec(sec):
        return pl.BlockSpec((None, n_heads, tile, LANES),
                            lambda b, s, l: (sec, 0, b * per_seq + s, 0))

    grid_spec = pltpu.PrefetchScalarGridSpec(
        num_scalar_prefetch=1,
        grid=(batch, per_seq),
        in_specs=[
            sec_spec(0), sec_spec(1), sec_spec(2), sec_spec(3),
            pl.BlockSpec(lbs.shape, lambda b, s, l: (0, 0, 0)),
            pl.BlockSpec((None, 1, HEAD_DIM), lambda b, s, l: (l[0], 0, 0)),
            pl.BlockSpec(level_mat.shape, lambda b, s, l: (0, 0)),
        ],
        out_specs=pl.BlockSpec((n_heads, tile, LANES), lambda b, s, l: (0, b * per_seq + s, 0)),
        scratch_shapes=[pltpu.VMEM((n_heads, HEAD_DIM, HEAD_DIM), F32),
                        pltpu.VMEM((HGRN_CHUNK, HGRN_CHUNK), jnp.int32),
                        pltpu.VMEM((2, tile, HEAD_DIM), BF16),
                        pltpu.VMEM((2, tile, HEAD_DIM), BF16),
                        pltpu.VMEM((level_mat.shape[0], n_chunks * HEAD_DIM), F32),
                        pltpu.VMEM((level_mat.shape[0], n_chunks * HEAD_DIM), F32)],
    )
    return pl.pallas_call(
        _hgrn_kernel,
        grid_spec=grid_spec,
        out_shape=jax.ShapeDtypeStruct((n_heads, t, LANES), BF16),
        compiler_params=_params(2),
        name="hgrn2",
    )(l, proj5, proj5, proj5, proj5, lbs, norm_g, level_mat)


def _pool_kernel(l_ref, u_ref, pw_ref, ps_ref, o_ref, halo_ref):
    grp = pl.program_id(1)
    s = pl.program_id(2)
    tile = u_ref.shape[1]
    width = jnp.left_shift(2, grp)

    @pl.when(s == 0)
    def _():
        halo_ref[...] = jnp.zeros_like(halo_ref)

    u = jnp.concatenate([u_ref[0], u_ref[1]], axis=1)
    ext = jnp.concatenate([halo_ref[...], u], axis=0)

    trow = lax.broadcasted_iota(jnp.int32, (POOL_HALO, 2 * POOL_HALO), 0) + POOL_HALO
    rcol = lax.broadcasted_iota(jnp.int32, (POOL_HALO, 2 * POOL_HALO), 1)
    band = ((rcol <= trow) & (rcol > trow - width)).astype(BF16)
    sums = jnp.concatenate(
        [jnp.dot(band, ext[r * POOL_HALO:(r + 2) * POOL_HALO], preferred_element_type=F32)
         for r in range(tile // POOL_HALO)], axis=0)

    pos = lax.broadcasted_iota(jnp.int32, (tile, 1), 0) + s * tile + 1
    count = jnp.minimum(pos, width).astype(F32)
    pooled = sums / count - u.astype(F32)
    y = jnp.dot(pooled.astype(BF16), pw_ref[...], preferred_element_type=F32)
    o_ref[...] = (y * ps_ref[...]).astype(BF16)
    halo_ref[...] = u[tile - POOL_HALO:, :]


def _pool(l, proj, pool_w, pool_scale, batch, seq):
    t = proj.shape[1]
    depth, groups, gd, _ = pool_w.shape
    tile = POOL_TILE
    per_seq = seq // tile
    blocks_per_group = gd // LANES
    first_block = proj.shape[0] - groups * blocks_per_group
    grid_spec = pltpu.PrefetchScalarGridSpec(
        num_scalar_prefetch=1,
        grid=(batch, groups, per_seq),
        in_specs=[
            pl.BlockSpec((blocks_per_group, tile, LANES),
                         lambda b, g, s, l: (first_block // blocks_per_group + g, b * per_seq + s, 0)),
            pl.BlockSpec((None, None, gd, gd), lambda b, g, s, l: (l[0], g, 0, 0)),
            pl.BlockSpec((None, None, 1, gd), lambda b, g, s, l: (l[0], g, 0, 0)),
        ],
        out_specs=pl.BlockSpec((tile, gd), lambda b, g, s, l: (b * per_seq + s, g)),
        scratch_shapes=[pltpu.VMEM((POOL_HALO, gd), BF16)],
    )
    return pl.pallas_call(
        _pool_kernel,
        grid_spec=grid_spec,
        out_shape=jax.ShapeDtypeStruct((t, groups * gd), BF16),
        compiler_params=_params(3),
        name="pool_mix",
    )(l, proj, pool_w, pool_scale.reshape(depth, groups, 1, gd))


def _outproj_kernel(l_ref, ya_ref, yb_ref, w_ref, x_ref, gate_ref, g_ref, sh_ref, sc_ref, o_ref, h_ref):
    y = jnp.concatenate([ya_ref[h] for h in range(ya_ref.shape[0])] + [yb_ref[...]], axis=1)
    acc = jnp.dot(y, w_ref[...], preferred_element_type=F32)
    x1 = x_ref[...] + gate_ref[...] * acc
    o_ref[...] = x1
    h_ref[...] = _norm_mod(x1, g_ref[...], sc_ref[...], sh_ref[...]).astype(BF16)


def _out_proj(l, ya, yb, w_out, x2d, gains, mods, seq):
    t, d = x2d.shape
    n_heads = ya.shape[0]
    tm = OUTPROJ_TM
    per_seq = seq // tm

    def mod_spec(k):
        return pl.BlockSpec((None, None, None, 1, d), lambda i, l: (l[0], i // per_seq, k, 0, 0))

    grid_spec = pltpu.PrefetchScalarGridSpec(
        num_scalar_prefetch=1,
        grid=(t // tm,),
        in_specs=[
            pl.BlockSpec((n_heads, tm, LANES), lambda i, l: (0, i, 0)),
            pl.BlockSpec((tm, yb.shape[1]), lambda i, l: (i, 0)),
            pl.BlockSpec((None, d, d), lambda i, l: (l[0], 0, 0)),
            pl.BlockSpec((tm, d), lambda i, l: (i, 0)),
            mod_spec(2),
            pl.BlockSpec((None, 1, d), lambda i, l: (l[0], 0, 0)),
            mod_spec(3), mod_spec(4),
        ],
        out_specs=[pl.BlockSpec((tm, d), lambda i, l: (i, 0)),
                   pl.BlockSpec((tm, d), lambda i, l: (i, 0))],
    )
    return pl.pallas_call(
        _outproj_kernel,
        grid_spec=grid_spec,
        out_shape=[jax.ShapeDtypeStruct((t, d), F32), jax.ShapeDtypeStruct((t, d), BF16)],
        compiler_params=_params(1),
        name="out_proj",
    )(l, ya, yb, w_out, x2d, mods, gains, mods, mods)


def _shift_rows(a, k, halo):
    rolled = pltpu.roll(a, k, axis=0)
    prev = pltpu.roll(halo, k, axis=0)
    ri = lax.broadcasted_iota(jnp.int32, halo.shape, 0)
    top = jnp.where(ri < k, prev, rolled[0:halo.shape[0]])
    return jnp.concatenate([top, rolled[halo.shape[0]:]], axis=0)


def _ffn_kernel(l_ref, x_ref, h_ref, gate_ref, fg_ref, wa_ref, wv_ref, cw_ref, cb_ref, wd_ref,
                o_ref, acc_ref, halo_ref, *, tiles_per_seq, depth):
    i = pl.program_id(0)
    j = pl.program_id(1)
    halo_rows = halo_ref.shape[1]

    @pl.when(j == 0)
    def _():
        acc_ref[...] = jnp.zeros_like(acc_ref)

    @pl.when(i % tiles_per_seq == 0)
    def _():
        halo_ref[j] = jnp.zeros(halo_ref.shape[1:], F32)

    h = h_ref[...]
    a = jnp.dot(h, wa_ref[...], preferred_element_type=F32)
    v = jnp.dot(h, wv_ref[...], preferred_element_type=F32)
    halo = halo_ref[j]
    cw = cw_ref[...]
    conv = cb_ref[...] + cw[CONV_WIDTH - 1:CONV_WIDTH] * a
    for back in range(1, CONV_WIDTH):
        tap = CONV_WIDTH - 1 - back
        conv = conv + cw[tap:tap + 1] * _shift_rows(a, back, halo)
    halo_ref[j] = a[a.shape[0] - halo_rows:, :]
    glu = (_silu(conv) * v).astype(BF16)
    acc_ref[...] += jnp.dot(glu, wd_ref[...], preferred_element_type=F32)

    last_step = j == pl.num_programs(1) - 1
    last_layer = l_ref[0] == depth - 1

    @pl.when(last_step & jnp.logical_not(last_layer))
    def _():
        o_ref[...] = x_ref[...] + gate_ref[...] * acc_ref[...]

    @pl.when(last_step & last_layer)
    def _():
        xf = x_ref[...] + gate_ref[...] * acc_ref[...]
        ms = jnp.mean(xf * xf, axis=-1, keepdims=True)
        o_ref[...] = xf * lax.rsqrt(ms + EPS) * fg_ref[...]


def _ffn(l, x2d, h2d, mods, final_g, w_up, conv_w, conv_b, w_down, seq):
    t, d = x2d.shape
    depth, ff, _ = w_down.shape
    tm, tf = FFN_TM, FFN_TF
    per_seq = seq // tm
    n_ff = ff // tf

    grid_spec = pltpu.PrefetchScalarGridSpec(
        num_scalar_prefetch=1,
        grid=(t // tm, n_ff),
        in_specs=[
            pl.BlockSpec((tm, d), lambda i, j, l: (i, 0)),
            pl.BlockSpec((tm, d), lambda i, j, l: (i, 0)),
            pl.BlockSpec((None, None, None, 1, d), lambda i, j, l: (l[0], i // per_seq, 5, 0, 0)),
            pl.BlockSpec((1, d), lambda i, j, l: (0, 0)),
            pl.BlockSpec((None, d, tf), lambda i, j, l: (l[0], 0, j)),
            pl.BlockSpec((None, d, tf), lambda i, j, l: (l[0], 0, n_ff + j)),
            pl.BlockSpec((None, CONV_WIDTH, tf), lambda i, j, l: (l[0], 0, j)),
            pl.BlockSpec((None, 1, tf), lambda i, j, l: (l[0], 0, j)),
            pl.BlockSpec((None, tf, d), lambda i, j, l: (l[0], j, 0)),
        ],
        out_specs=pl.BlockSpec((tm, d), lambda i, j, l: (i, 0)),
        scratch_shapes=[
            pltpu.VMEM((tm, d), F32),
            pltpu.VMEM((n_ff, 8, tf), F32),
        ],
    )
    return pl.pallas_call(
        functools.partial(_ffn_kernel, tiles_per_seq=per_seq, depth=depth),
        grid_spec=grid_spec,
        out_shape=jax.ShapeDtypeStruct((t, d), F32),
        compiler_params=_params(2),
        name="conv_glu_ffn",
    )(l, x2d, h2d, mods, final_g, w_up, w_up, conv_w, conv_b.reshape(depth, 1, ff), w_down)


def kernel(x, c, ada_w, ada_b, mix_norm_g, w_in, hgrn_lower_bounds, hgrn_norm_g, pool_w, pool_scale,
           w_out, ffn_norm_g, w_up, conv_w, conv_b, w_down, final_norm_g):
    batch, seq, d = x.shape
    depth = w_in.shape[0]
    d_hgrn = hgrn_lower_bounds.shape[1]
    n_heads = d_hgrn // HEAD_DIM

    mods = _modulation(c, ada_w, ada_b).reshape(depth, batch, N_MOD, 1, d)
    w_in_b, w_out_b = w_in.astype(BF16), w_out.astype(BF16)
    w_up_b, w_down_b, pool_w_b = w_up.astype(BF16), w_down.astype(BF16), pool_w.astype(BF16)
    lbs = hgrn_lower_bounds.reshape(depth, n_heads, HEAD_DIM).transpose(1, 0, 2)
    mix_g = mix_norm_g.reshape(depth, 1, d)
    ffn_g = ffn_norm_g.reshape(depth, 1, d)
    head_g = hgrn_norm_g.reshape(depth, 1, HEAD_DIM)
    final_g = final_norm_g.reshape(1, d)
    level_mat = jnp.asarray(_hgrn_level_matrix(HGRN_CHUNK), dtype=BF16)

    xc = x.reshape(batch * seq, d)
    for li in range(depth):
        l = jnp.full((1,), li, jnp.int32)
        proj = _in_proj(l, xc, mix_g, mods, w_in_b, seq)
        ya = _hgrn(l, proj, lbs, head_g, level_mat, batch, seq)
        yb = _pool(l, proj, pool_w_b, pool_scale, batch, seq)
        x1, h2 = _out_proj(l, ya, yb, w_out_b, xc, ffn_g, mods, seq)
        xc = _ffn(l, x1, h2, mods, final_g, w_up_b, conv_w, conv_b, w_down_b, seq)
    return xc.reshape(batch, seq, d)
```

```python
import functools

import jax
import jax.numpy as jnp
import numpy as np
from jax import lax
from jax.experimental import pallas as pl
from jax.experimental.pallas import tpu as pltpu

F32 = jnp.float32
BF16 = jnp.bfloat16

LANES = 128
HEAD_DIM = 128
POOL_WINDOWS = (2, 4, 8, 16)
POOL_HALO = 128
CONV_WIDTH = 3
N_MOD = 6
EPS = 1e-6
LOG2E = 1.4426950408889634
VMEM_LIMIT = 56 * 1024 * 1024

HGRN_CHUNK = 128
HGRN_TILE = 1024
POOL_TILE = 1024
INPROJ_TM, INPROJ_TN = 512, 1024
OUTPROJ_TM = 512
FFN_TM, FFN_TF = 512, 512
MOD_TN = 1024


def _params(n_axes, vmem=VMEM_LIMIT):
    return pltpu.CompilerParams(
        dimension_semantics=("arbitrary",) * n_axes, vmem_limit_bytes=vmem)


def _norm_mod(xf, gain, scale, shift):
    ms = jnp.mean(xf * xf, axis=-1, keepdims=True)
    y = xf * lax.rsqrt(ms + EPS) * gain
    return y * (1.0 + scale) + shift


def _silu(x):
    return (0.5 * x) * (1.0 + jnp.tanh(0.5 * x))


def _mod_kernel(c_ref, w_ref, b_ref, o_ref):
    c = c_ref[...]
    c_act = _silu(c).astype(BF16)
    acc = jnp.dot(c_act, w_ref[...].astype(BF16), preferred_element_type=F32)
    o_ref[...] = acc + b_ref[...]


def _modulation(c, ada_w, ada_b):
    depth, d, n = ada_w.shape
    b = c.shape[0]
    return pl.pallas_call(
        _mod_kernel,
        grid=(depth, n // MOD_TN),
        in_specs=[
            pl.BlockSpec((b, d), lambda l, j: (0, 0)),
            pl.BlockSpec((None, d, MOD_TN), lambda l, j: (l, 0, j)),
            pl.BlockSpec((None, 1, MOD_TN), lambda l, j: (l, 0, j)),
        ],
        out_specs=pl.BlockSpec((None, b, MOD_TN), lambda l, j: (l, 0, j)),
        out_shape=jax.ShapeDtypeStruct((depth, b, n), F32),
        compiler_params=_params(2),
        name="ada_mod",
    )(c, ada_w, ada_b.reshape(depth, 1, n))


def _inproj_kernel(l_ref, x_ref, g_ref, sh_ref, sc_ref, w_ref, o_ref):
    h = _norm_mod(x_ref[...], g_ref[...], sc_ref[...], sh_ref[...]).astype(BF16)
    blocks_per_dot = INPROJ_TN // LANES
    for j in range(0, o_ref.shape[0], blocks_per_dot):
        acc = jnp.dot(h, w_ref[:, j * LANES:(j + blocks_per_dot) * LANES], preferred_element_type=F32)
        for blk in range(blocks_per_dot):
            o_ref[j + blk] = acc[:, blk * LANES:(blk + 1) * LANES].astype(BF16)


def _in_proj(l, x2d, gains, mods, w_in, seq):
    t, d = x2d.shape
    n = w_in.shape[2]
    tm = INPROJ_TM
    per_seq = seq // tm
    grid_spec = pltpu.PrefetchScalarGridSpec(
        num_scalar_prefetch=1,
        grid=(t // tm,),
        in_specs=[
            pl.BlockSpec((tm, d), lambda i, l: (i, 0)),
            pl.BlockSpec((None, 1, d), lambda i, l: (l[0], 0, 0)),
            pl.BlockSpec((None, None, None, 1, d), lambda i, l: (l[0], i // per_seq, 0, 0, 0)),
            pl.BlockSpec((None, None, None, 1, d), lambda i, l: (l[0], i // per_seq, 1, 0, 0)),
            pl.BlockSpec((None, d, n), lambda i, l: (l[0], 0, 0), pipeline_mode=pl.Buffered(1)),
        ],
        out_specs=pl.BlockSpec((n // LANES, tm, LANES), lambda i, l: (0, i, 0)),
    )
    return pl.pallas_call(
        _inproj_kernel,
        grid_spec=grid_spec,
        out_shape=jax.ShapeDtypeStruct((n // LANES, t, LANES), BF16),
        compiler_params=_params(1),
        name="in_proj",
    )(l, x2d, gains, mods, mods, w_in)


def _hgrn_level_matrix(chunk):
    t = np.arange(chunk)[:, None]
    r = np.arange(chunk)[None, :]
    low = (r <= t).astype(np.float32)
    mats = [low]
    m = 1
    while m < chunk:
        anchor = (t // (2 * m)) * (2 * m) + m - 1
        sign = np.where(t > anchor, 1.0, -1.0)
        mats.append(sign * (low - (r <= anchor).astype(np.float32)))
        m *= 2
    stacked = np.concatenate(mats, axis=0)
    return np.concatenate([stacked] * 2, axis=1)


_NT = (((1,), (1,)), ((), ()))
_TN = (((0,), (0,)), ((), ()))


def _hgrn_kernel(l_ref, q_ref, f_ref, i_ref, g_ref, lb_ref, ng_ref, lm_ref, o_ref,
                 st_ref, lvl_ref, qk0_ref, qk1_ref, bd0_ref, bd1_ref):
    chunk = HGRN_CHUNK
    n_levels = chunk.bit_length() - 1
    n_heads, tile, _ = q_ref.shape
    n_chunks = tile // chunk
    layer = l_ref[0]

    @pl.when((pl.program_id(0) == 0) & (pl.program_id(1) == 0))
    def _():
        trow = lax.broadcasted_iota(jnp.int32, (chunk, chunk), 0)
        tcol = lax.broadcasted_iota(jnp.int32, (chunk, chunk), 1)
        txor = trow ^ tcol
        top = jnp.full((chunk, chunk), -1, jnp.int32)
        for k in range(n_levels):
            top = top + ((txor >> k) != 0).astype(jnp.int32)
        lvl_ref[...] = jnp.where(trow > tcol, top, jnp.where(trow == tcol, n_levels, -1))

    @pl.when(pl.program_id(1) == 0)
    def _():
        st_ref[...] = jnp.zeros_like(st_ref)

    gain = ng_ref[...]

    def gates(h, qk_ref, bd_ref):
        lbraw = lb_ref[h]
        ex = jnp.exp(lbraw - jnp.max(lbraw, axis=0, keepdims=True))
        p = ex / jnp.sum(ex, axis=0, keepdims=True)
        lrow = lax.broadcasted_iota(jnp.int32, p.shape, 0)
        cum = jnp.sum(jnp.where(lrow <= layer, p, 0.0), axis=0, keepdims=True)
        lb = jnp.clip(cum - p[0:1], 0.0, 1.0)
        half_key = 0.5 * (1.0 - lb)

        parts = []
        for c in range(n_chunks):
            rows = pl.ds(c * chunk, chunk)
            z = f_ref[h, rows, :].astype(F32)
            e = jnp.exp2(z * (-LOG2E))
            logf = (jnp.log(1.0 + lb * e) - jnp.log(1.0 + e)) * LOG2E
            key = half_key - half_key * jnp.tanh(0.5 * z)
            qraw = q_ref[h, rows, :].astype(F32)
            qq = (qraw * (0.5 * HEAD_DIM ** -0.5)) * (1.0 + jnp.tanh(0.5 * qraw))
            qk_ref[0, rows, :] = qq.astype(BF16)
            qk_ref[1, rows, :] = key.astype(BF16)
            hi = logf.astype(BF16)
            lo = (logf - hi.astype(F32)).astype(BF16)
            parts.append(jnp.concatenate([hi, lo], axis=0))

        bd_ref[...] = jnp.dot(lm_ref[...], jnp.concatenate(parts, axis=1), preferred_element_type=F32)

    def mix(h, qk_ref, bd_ref):
        st = st_ref[h]
        lvl = lvl_ref[...]
        all_scores = []
        for c in range(n_chunks):
            rows = pl.ds(c * chunk, chunk)
            s0 = lax.dot_general(qk_ref[0, rows, :], qk_ref[1, rows, :], _NT, preferred_element_type=F32)
            all_scores.append(jnp.where(lvl == n_levels, s0, 0.0))
        for k in range(n_levels):
            for c in range(n_chunks):
                rows = pl.ds(c * chunk, chunk)
                cols = pl.ds(c * HEAD_DIM, HEAD_DIM)
                w = jnp.exp2(bd_ref[pl.ds(chunk * (k + 1), chunk), cols]).astype(BF16)
                sk = lax.dot_general(qk_ref[0, rows, :] * w, qk_ref[1, rows, :] * w, _NT,
                                     preferred_element_type=F32)
                all_scores[c] = jnp.where(lvl == k, sk, all_scores[c])
        all_scores = [s.astype(BF16) for s in all_scores]

        for c in range(n_chunks):
            rows = pl.ds(c * chunk, chunk)
            cols = pl.ds(c * HEAD_DIM, HEAD_DIM)
            qq = qk_ref[0, rows, :]
            key = qk_ref[1, rows, :]
            val = i_ref[h, rows, :]
            b = bd_ref[pl.ds(0, chunk), cols]
            o = jnp.dot(all_scores[c], val, preferred_element_type=F32)
            o = o + lax.dot_general(qq * jnp.exp2(b).astype(BF16), st.astype(BF16), _NT,
                                    preferred_element_type=F32)

            b_last = b[chunk - 1:chunk]
            k_out = key * jnp.exp2(b_last - b).astype(BF16)
            st = st * jnp.exp2(b_last) + lax.dot_general(val, k_out, _TN, preferred_element_type=F32)

            ms = jnp.mean(o * o, axis=-1, keepdims=True)
            on = o * lax.rsqrt(ms + EPS) * gain
            graw = g_ref[h, rows, :].astype(F32)
            o_ref[h, rows, :] = (on * _silu(graw)).astype(BF16)
        st_ref[h] = st

    gates(0, qk0_ref, bd0_ref)

    def pair_body(j, carry):
        h0 = 2 * j
        mix(h0, qk0_ref, bd0_ref)
        gates(h0 + 1, qk1_ref, bd1_ref)
        mix(h0 + 1, qk1_ref, bd1_ref)
        gates(jnp.minimum(h0 + 2, n_heads - 1), qk0_ref, bd0_ref)
        return carry

    lax.fori_loop(0, n_heads // 2, pair_body, 0)


def _hgrn(l, proj, lbs, norm_g, level_mat, batch, seq):
    n_heads = lbs.shape[0]
    t = proj.shape[1]
    tile = HGRN_TILE
    per_seq = seq // tile
    n_chunks = tile // HGRN_CHUNK
    proj5 = proj.reshape(5, n_heads, t, LANES)

    def sec_spec(sec):
        return pl.BlockSpec((None, n_heads, tile, LANES),
                            lambda b, s, l: (sec, 0, b * per_seq + s, 0))

    grid_spec = pltpu.PrefetchScalarGridSpec(
        num_scalar_prefetch=1,
        grid=(batch, per_seq),
        in_specs=[
            sec_spec(0), sec_spec(1), sec_spec(2), sec_spec(3),
            pl.BlockSpec(lbs.shape, lambda b, s, l: (0, 0, 0)),
            pl.BlockSpec((None, 1, HEAD_DIM), lambda b, s, l: (l[0], 0, 0)),
            pl.BlockSpec(level_mat.shape, lambda b, s, l: (0, 0)),
        ],
        out_specs=pl.BlockSpec((n_heads, tile, LANES), lambda b, s, l: (0, b * per_seq + s, 0)),
        scratch_shapes=[pltpu.VMEM((n_heads, HEAD_DIM, HEAD_DIM), F32),
                        pltpu.VMEM((HGRN_CHUNK, HGRN_CHUNK), jnp.int32),
                        pltpu.VMEM((2, tile, HEAD_DIM), BF16),
                        pltpu.VMEM((2, tile, HEAD_DIM), BF16),
                        pltpu.VMEM((level_mat.shape[0], n_chunks * HEAD_DIM), F32),
                        pltpu.VMEM((level_mat.shape[0], n_chunks * HEAD_DIM), F32)],
    )
    return pl.pallas_call(
        _hgrn_kernel,
        grid_spec=grid_spec,
        out_shape=jax.ShapeDtypeStruct((n_heads, t, LANES), BF16),
        compiler_params=_params(2),
        name="hgrn2",
    )(l, proj5, proj5, proj5, proj5, lbs, norm_g, level_mat)


def _pool_kernel(l_ref, u_ref, pw_ref, ps_ref, o_ref, halo_ref):
    grp = pl.program_id(1)
    s = pl.program_id(2)
    tile = u_ref.shape[1]
    width = jnp.left_shift(2, grp)

    @pl.when(s == 0)
    def _():
        halo_ref[...] = jnp.zeros_like(halo_ref)

    u = jnp.concatenate([u_ref[0], u_ref[1]], axis=1)
    ext = jnp.concatenate([halo_ref[...], u], axis=0)

    trow = lax.broadcasted_iota(jnp.int32, (POOL_HALO, 2 * POOL_HALO), 0) + POOL_HALO
    rcol = lax.broadcasted_iota(jnp.int32, (POOL_HALO, 2 * POOL_HALO), 1)
    band = ((rcol <= trow) & (rcol > trow - width)).astype(BF16)
    sums = jnp.concatenate(
        [jnp.dot(band, ext[r * POOL_HALO:(r + 2) * POOL_HALO], preferred_element_type=F32)
         for r in range(tile // POOL_HALO)], axis=0)

    pos = lax.broadcasted_iota(jnp.int32, (tile, 1), 0) + s * tile + 1
    count = jnp.minimum(pos, width).astype(F32)
    pooled = sums / count - u.astype(F32)
    y = jnp.dot(pooled.astype(BF16), pw_ref[...], preferred_element_type=F32)
    o_ref[...] = (y * ps_ref[...]).astype(BF16)
    halo_ref[...] = u[tile - POOL_HALO:, :]


def _pool(l, proj, pool_w, pool_scale, batch, seq):
    t = proj.shape[1]
    depth, groups, gd, _ = pool_w.shape
    tile = POOL_TILE
    per_seq = seq // tile
    blocks_per_group = gd // LANES
    first_block = proj.shape[0] - groups * blocks_per_group
    grid_spec = pltpu.PrefetchScalarGridSpec(
        num_scalar_prefetch=1,
        grid=(batch, groups, per_seq),
        in_specs=[
            pl.BlockSpec((blocks_per_group, tile, LANES),
                         lambda b, g, s, l: (first_block // blocks_per_group + g, b * per_seq + s, 0)),
            pl.BlockSpec((None, None, gd, gd), lambda b, g, s, l: (l[0], g, 0, 0)),
            pl.BlockSpec((None, None, 1, gd), lambda b, g, s, l: (l[0], g, 0, 0)),
        ],
        out_specs=pl.BlockSpec((tile, gd), lambda b, g, s, l: (b * per_seq + s, g)),
        scratch_shapes=[pltpu.VMEM((POOL_HALO, gd), BF16)],
    )
    return pl.pallas_call(
        _pool_kernel,
        grid_spec=grid_spec,
        out_shape=jax.ShapeDtypeStruct((t, groups * gd), BF16),
        compiler_params=_params(3),
        name="pool_mix",
    )(l, proj, pool_w, pool_scale.reshape(depth, groups, 1, gd))


def _outproj_kernel(l_ref, ya_ref, yb_ref, w_ref, x_ref, gate_ref, g_ref, sh_ref, sc_ref, o_ref, h_ref):
    y = jnp.concatenate([ya_ref[h] for h in range(ya_ref.shape[0])] + [yb_ref[...]], axis=1)
    acc = jnp.dot(y, w_ref[...], preferred_element_type=F32)
    x1 = x_ref[...] + gate_ref[...] * acc
    o_ref[...] = x1
    h_ref[...] = _norm_mod(x1, g_ref[...], sc_ref[...], sh_ref[...]).astype(BF16)


def _out_proj(l, ya, yb, w_out, x2d, gains, mods, seq):
    t, d = x2d.shape
    n_heads = ya.shape[0]
    tm = OUTPROJ_TM
    per_seq = seq // tm

    def mod_spec(k):
        return pl.BlockSpec((None, None, None, 1, d), lambda i, l: (l[0], i // per_seq, k, 0, 0))

    grid_spec = pltpu.PrefetchScalarGridSpec(
        num_scalar_prefetch=1,
        grid=(t // tm,),
        in_specs=[
            pl.BlockSpec((n_heads, tm, LANES), lambda i, l: (0, i, 0)),
            pl.BlockSpec((tm, yb.shape[1]), lambda i, l: (i, 0)),
            pl.BlockSpec((None, d, d), lambda i, l: (l[0], 0, 0)),
            pl.BlockSpec((tm, d), lambda i, l: (i, 0)),
            mod_spec(2),
            pl.BlockSpec((None, 1, d), lambda i, l: (l[0], 0, 0)),
            mod_spec(3), mod_spec(4),
        ],
        out_specs=[pl.BlockSpec((tm, d), lambda i, l: (i, 0)),
                   pl.BlockSpec((tm, d), lambda i, l: (i, 0))],
    )
    return pl.pallas_call(
        _outproj_kernel,
        grid_spec=grid_spec,
        out_shape=[jax.ShapeDtypeStruct((t, d), F32), jax.ShapeDtypeStruct((t, d), BF16)],
        compiler_params=_params(1),
        name="out_proj",
    )(l, ya, yb, w_out, x2d, mods, gains, mods, mods)


def _shift_rows(a, k, halo):
    rolled = pltpu.roll(a, k, axis=0)
    prev = pltpu.roll(halo, k, axis=0)
    ri = lax.broadcasted_iota(jnp.int32, halo.shape, 0)
    top = jnp.where(ri < k, prev, rolled[0:halo.shape[0]])
    return jnp.concatenate([top, rolled[halo.shape[0]:]], axis=0)


def _ffn_kernel(l_ref, x_ref, h_ref, gate_ref, fg_ref, wa_ref, wv_ref, cw_ref, cb_ref, wd_ref,
                o_ref, acc_ref, halo_ref, *, tiles_per_seq, depth):
    i = pl.program_id(0)
    j = pl.program_id(1)
    halo_rows = halo_ref.shape[1]

    @pl.when(j == 0)
    def _():
        acc_ref[...] = jnp.zeros_like(acc_ref)

    @pl.when(i % tiles_per_seq == 0)
    def _():
        halo_ref[j] = jnp.zeros(halo_ref.shape[1:], F32)

    h = h_ref[...]
    a = jnp.dot(h, wa_ref[...], preferred_element_type=F32)
    v = jnp.dot(h, wv_ref[...], preferred_element_type=F32)
    halo = halo_ref[j]
    cw = cw_ref[...]
    conv = cb_ref[...] + cw[CONV_WIDTH - 1:CONV_WIDTH] * a
    for back in range(1, CONV_WIDTH):
        tap = CONV_WIDTH - 1 - back
        conv = conv + cw[tap:tap + 1] * _shift_rows(a, back, halo)
    halo_ref[j] = a[a.shape[0] - halo_rows:, :]
    glu = (_silu(conv) * v).astype(BF16)
    acc_ref[...] += jnp.dot(glu, wd_ref[...], preferred_element_type=F32)

    last_step = j == pl.num_programs(1) - 1
    last_layer = l_ref[0] == depth - 1

    @pl.when(last_step & jnp.logical_not(last_layer))
    def _():
        o_ref[...] = x_ref[...] + gate_ref[...] * acc_ref[...]

    @pl.when(last_step & last_layer)
    def _():
        xf = x_ref[...] + gate_ref[...] * acc_ref[...]
        ms = jnp.mean(xf * xf, axis=-1, keepdims=True)
        o_ref[...] = xf * lax.rsqrt(ms + EPS) * fg_ref[...]


def _ffn(l, x2d, h2d, mods, final_g, w_up, conv_w, conv_b, w_down, seq):
    t, d = x2d.shape
    depth, ff, _ = w_down.shape
    tm, tf = FFN_TM, FFN_TF
    per_seq = seq // tm
    n_ff = ff // tf

    grid_spec = pltpu.PrefetchScalarGridSpec(
        num_scalar_prefetch=1,
        grid=(t // tm, n_ff),
        in_specs=[
            pl.BlockSpec((tm, d), lambda i, j, l: (i, 0)),
            pl.BlockSpec((tm, d), lambda i, j, l: (i, 0)),
            pl.BlockSpec((None, None, None, 1, d), lambda i, j, l: (l[0], i // per_seq, 5, 0, 0)),
            pl.BlockSpec((1, d), lambda i, j, l: (0, 0)),
            pl.BlockSpec((None, d, tf), lambda i, j, l: (l[0], 0, j)),
            pl.BlockSpec((None, d, tf), lambda i, j, l: (l[0], 0, n_ff + j)),
            pl.BlockSpec((None, CONV_WIDTH, tf), lambda i, j, l: (l[0], 0, j)),
            pl.BlockSpec((None, 1, tf), lambda i, j, l: (l[0], 0, j)),
            pl.BlockSpec((None, tf, d), lambda i, j, l: (l[0], j, 0)),
        ],
        out_specs=pl.BlockSpec((tm, d), lambda i, j, l: (i, 0)),
        scratch_shapes=[
            pltpu.VMEM((tm, d), F32),
            pltpu.VMEM((n_ff, 8, tf), F32),
        ],
    )
    return pl.pallas_call(
        functools.partial(_ffn_kernel, tiles_per_seq=per_seq, depth=depth),
        grid_spec=grid_spec,
        out_shape=jax.ShapeDtypeStruct((t, d), F32),
        compiler_params=_params(2),
        name="conv_glu_ffn",
    )(l, x2d, h2d, mods, final_g, w_up, w_up, conv_w, conv_b.reshape(depth, 1, ff), w_down)


def kernel(x, c, ada_w, ada_b, mix_norm_g, w_in, hgrn_lower_bounds, hgrn_norm_g, pool_w, pool_scale,
           w_out, ffn_norm_g, w_up, conv_w, conv_b, w_down, final_norm_g):
    batch, seq, d = x.shape
    depth = w_in.shape[0]
    d_hgrn = hgrn_lower_bounds.shape[1]
    n_heads = d_hgrn // HEAD_DIM

    mods = _modulation(c, ada_w, ada_b).reshape(depth, batch, N_MOD, 1, d)
    w_in_b, w_out_b = w_in.astype(BF16), w_out.astype(BF16)
    w_up_b, w_down_b, pool_w_b = w_up.astype(BF16), w_down.astype(BF16), pool_w.astype(BF16)
    lbs = hgrn_lower_bounds.reshape(depth, n_heads, HEAD_DIM).transpose(1, 0, 2)
    mix_g = mix_norm_g.reshape(depth, 1, d)
    ffn_g = ffn_norm_g.reshape(depth, 1, d)
    head_g = hgrn_norm_g.reshape(depth, 1, HEAD_DIM)
    final_g = final_norm_g.reshape(1, d)
    level_mat = jnp.asarray(_hgrn_level_matrix(HGRN_CHUNK), dtype=BF16)

    xc = x.reshape(batch * seq, d)
    for li in range(depth):
        l = jnp.full((1,), li, jnp.int32)
        proj = _in_proj(l, xc, mix_g, mods, w_in_b, seq)
        ya = _hgrn(l, proj, lbs, head_g, level_mat, batch, seq)
        yb = _pool(l, proj, pool_w_b, pool_scale, batch, seq)
        x1, h2 = _out_proj(l, ya, yb, w_out_b, xc, ffn_g, mods, seq)
        xc = _ffn(l, x1, h2, mods, final_g, w_up_b, conv_w, conv_b, w_down_b, seq)
    return xc.reshape(batch, seq, d)
```

```python
import functools

import jax
import jax.numpy as jnp
import numpy as np
from jax import lax
from jax.experimental import pallas as pl
from jax.experimental.pallas import tpu as pltpu

F32 = jnp.float32
BF16 = jnp.bfloat16

LANES = 128
HEAD_DIM = 128
POOL_WINDOWS = (2, 4, 8, 16)
POOL_HALO = 128
CONV_WIDTH = 3
N_MOD = 6
EPS = 1e-6
LOG2E = 1.4426950408889634
VMEM_LIMIT = 56 * 1024 * 1024

HGRN_CHUNK = 128
HGRN_TILE = 1024
POOL_TILE = 1024
INPROJ_TM, INPROJ_TN = 512, 1024
OUTPROJ_TM = 512
FFN_TM, FFN_TF = 1024, 512
MOD_TN = 1024


def _params(n_axes, vmem=VMEM_LIMIT):
    return pltpu.CompilerParams(
        dimension_semantics=("arbitrary",) * n_axes, vmem_limit_bytes=vmem)


def _norm_mod(xf, gain, scale, shift):
    ms = jnp.mean(xf * xf, axis=-1, keepdims=True)
    y = xf * lax.rsqrt(ms + EPS) * gain
    return y * (1.0 + scale) + shift


def _silu(x):
    return (0.5 * x) * (1.0 + jnp.tanh(0.5 * x))


def _mod_kernel(c_ref, w_ref, b_ref, o_ref):
    c = c_ref[...]
    c_act = _silu(c).astype(BF16)
    acc = jnp.dot(c_act, w_ref[...].astype(BF16), preferred_element_type=F32)
    o_ref[...] = acc + b_ref[...]


def _modulation(c, ada_w, ada_b):
    depth, d, n = ada_w.shape
    b = c.shape[0]
    return pl.pallas_call(
        _mod_kernel,
        grid=(depth, n // MOD_TN),
        in_specs=[
            pl.BlockSpec((b, d), lambda l, j: (0, 0)),
            pl.BlockSpec((None, d, MOD_TN), lambda l, j: (l, 0, j)),
            pl.BlockSpec((None, 1, MOD_TN), lambda l, j: (l, 0, j)),
        ],
        out_specs=pl.BlockSpec((None, b, MOD_TN), lambda l, j: (l, 0, j)),
        out_shape=jax.ShapeDtypeStruct((depth, b, n), F32),
        compiler_params=_params(2),
        name="ada_mod",
    )(c, ada_w, ada_b.reshape(depth, 1, n))


def _inproj_kernel(l_ref, x_ref, g_ref, sh_ref, sc_ref, w_ref, o_ref):
    h = _norm_mod(x_ref[...], g_ref[...], sc_ref[...], sh_ref[...]).astype(BF16)
    blocks_per_dot = INPROJ_TN // LANES
    for j in range(0, o_ref.shape[0], blocks_per_dot):
        acc = jnp.dot(h, w_ref[:, j * LANES:(j + blocks_per_dot) * LANES], preferred_element_type=F32)
        for blk in range(blocks_per_dot):
            o_ref[j + blk] = acc[:, blk * LANES:(blk + 1) * LANES].astype(BF16)


def _in_proj(l, x2d, gains, mods, w_in, seq):
    t, d = x2d.shape
    n = w_in.shape[2]
    tm = INPROJ_TM
    per_seq = seq // tm
    grid_spec = pltpu.PrefetchScalarGridSpec(
        num_scalar_prefetch=1,
        grid=(t // tm,),
        in_specs=[
            pl.BlockSpec((tm, d), lambda i, l: (i, 0)),
            pl.BlockSpec((None, 1, d), lambda i, l: (l[0], 0, 0)),
            pl.BlockSpec((None, None, None, 1, d), lambda i, l: (l[0], i // per_seq, 0, 0, 0)),
            pl.BlockSpec((None, None, None, 1, d), lambda i, l: (l[0], i // per_seq, 1, 0, 0)),
            pl.BlockSpec((None, d, n), lambda i, l: (l[0], 0, 0), pipeline_mode=pl.Buffered(1)),
        ],
        out_specs=pl.BlockSpec((n // LANES, tm, LANES), lambda i, l: (0, i, 0)),
    )
    return pl.pallas_call(
        _inproj_kernel,
        grid_spec=grid_spec,
        out_shape=jax.ShapeDtypeStruct((n // LANES, t, LANES), BF16),
        compiler_params=_params(1),
        name="in_proj",
    )(l, x2d, gains, mods, mods, w_in)


def _hgrn_level_matrix(chunk):
    t = np.arange(chunk)[:, None]
    r = np.arange(chunk)[None, :]
    low = (r <= t).astype(np.float32)
    mats = [low]
    m = 1
    while m < chunk:
        anchor = (t // (2 * m)) * (2 * m) + m - 1
        sign = np.where(t > anchor, 1.0, -1.0)
        mats.append(sign * (low - (r <= anchor).astype(np.float32)))
        m *= 2
    stacked = np.concatenate(mats, axis=0)
    return np.concatenate([stacked] * 2, axis=1)


_NT = (((1,), (1,)), ((), ()))
_TN = (((0,), (0,)), ((), ()))


def _hgrn_kernel(l_ref, q_ref, f_ref, i_ref, g_ref, lb_ref, ng_ref, lm_ref, o_ref,
                 st_ref, lvl_ref, qk0_ref, qk1_ref, bd0_ref, bd1_ref):
    chunk = HGRN_CHUNK
    n_levels = chunk.bit_length() - 1
    n_heads, tile, _ = q_ref.shape
    n_chunks = tile // chunk
    layer = l_ref[0]

    @pl.when((pl.program_id(0) == 0) & (pl.program_id(1) == 0))
    def _():
        trow = lax.broadcasted_iota(jnp.int32, (chunk, chunk), 0)
        tcol = lax.broadcasted_iota(jnp.int32, (chunk, chunk), 1)
        txor = trow ^ tcol
        top = jnp.full((chunk, chunk), -1, jnp.int32)
        for k in range(n_levels):
            top = top + ((txor >> k) != 0).astype(jnp.int32)
        lvl_ref[...] = jnp.where(trow > tcol, top, jnp.where(trow == tcol, n_levels, -1))

    @pl.when(pl.program_id(1) == 0)
    def _():
        st_ref[...] = jnp.zeros_like(st_ref)

    gain = ng_ref[...]

    def gates(h, qk_ref, bd_ref):
        lbraw = lb_ref[h]
        ex = jnp.exp(lbraw - jnp.max(lbraw, axis=0, keepdims=True))
        p = ex / jnp.sum(ex, axis=0, keepdims=True)
        lrow = lax.broadcasted_iota(jnp.int32, p.shape, 0)
        cum = jnp.sum(jnp.where(lrow <= layer, p, 0.0), axis=0, keepdims=True)
        lb = jnp.clip(cum - p[0:1], 0.0, 1.0)
        half_key = 0.5 * (1.0 - lb)

        parts = []
        for c in range(n_chunks):
            rows = pl.ds(c * chunk, chunk)
            z = f_ref[h, rows, :].astype(F32)
            e = jnp.exp2(z * (-LOG2E))
            logf = (jnp.log(1.0 + lb * e) - jnp.log(1.0 + e)) * LOG2E
            key = half_key - half_key * jnp.tanh(0.5 * z)
            qraw = q_ref[h, rows, :].astype(F32)
            qq = (qraw * (0.5 * HEAD_DIM ** -0.5)) * (1.0 + jnp.tanh(0.5 * qraw))
            qk_ref[0, rows, :] = qq.astype(BF16)
            qk_ref[1, rows, :] = key.astype(BF16)
            hi = logf.astype(BF16)
            lo = (logf - hi.astype(F32)).astype(BF16)
            parts.append(jnp.concatenate([hi, lo], axis=0))

        bd_ref[...] = jnp.dot(lm_ref[...], jnp.concatenate(parts, axis=1), preferred_element_type=F32)

    def mix(h, qk_ref, bd_ref):
        st = st_ref[h]
        lvl = lvl_ref[...]
        all_scores = []
        for c in range(n_chunks):
            rows = pl.ds(c * chunk, chunk)
            s0 = lax.dot_general(qk_ref[0, rows, :], qk_ref[1, rows, :], _NT, preferred_element_type=F32)
            all_scores.append(jnp.where(lvl == n_levels, s0, 0.0))
        for k in range(n_levels):
            for c in range(n_chunks):
                rows = pl.ds(c * chunk, chunk)
                cols = pl.ds(c * HEAD_DIM, HEAD_DIM)
                w = jnp.exp2(bd_ref[pl.ds(chunk * (k + 1), chunk), cols]).astype(BF16)
                sk = lax.dot_general(qk_ref[0, rows, :] * w, qk_ref[1, rows, :] * w, _NT,
                                     preferred_element_type=F32)
                all_scores[c] = jnp.where(lvl == k, sk, all_scores[c])
        all_scores = [s.astype(BF16) for s in all_scores]

        for c in range(n_chunks):
            rows = pl.ds(c * chunk, chunk)
            cols = pl.ds(c * HEAD_DIM, HEAD_DIM)
            qq = qk_ref[0, rows, :]
            key = qk_ref[1, rows, :]
            val = i_ref[h, rows, :]
            b = bd_ref[pl.ds(0, chunk), cols]
            o = jnp.dot(all_scores[c], val, preferred_element_type=F32)
            o = o + lax.dot_general(qq * jnp.exp2(b).astype(BF16), st.astype(BF16), _NT,
                                    preferred_element_type=F32)

            b_last = b[chunk - 1:chunk]
            k_out = key * jnp.exp2(b_last - b).astype(BF16)
            st = st * jnp.exp2(b_last) + lax.dot_general(val, k_out, _TN, preferred_element_type=F32)

            ms = jnp.mean(o * o, axis=-1, keepdims=True)
            on = o * lax.rsqrt(ms + EPS) * gain
            graw = g_ref[h, rows, :].astype(F32)
            o_ref[h, rows, :] = (on * _silu(graw)).astype(BF16)
        st_ref[h] = st

    gates(0, qk0_ref, bd0_ref)

    def pair_body(j, carry):
        h0 = 2 * j
        mix(h0, qk0_ref, bd0_ref)
        gates(h0 + 1, qk1_ref, bd1_ref)
        mix(h0 + 1, qk1_ref, bd1_ref)
        gates(jnp.minimum(h0 + 2, n_heads - 1), qk0_ref, bd0_ref)
        return carry

    lax.fori_loop(0, n_heads // 2, pair_body, 0)


def _hgrn(l, proj, lbs, norm_g, level_mat, batch, seq):
    n_heads = lbs.shape[0]
    t = proj.shape[1]
    tile = HGRN_TILE
    per_seq = seq // tile
    n_chunks = tile // HGRN_CHUNK
    proj5 = proj.reshape(5, n_heads, t, LANES)

    def sec_spec(sec):
        return pl.BlockSpec((None, n_heads, tile, LANES),
                            lambda b, s, l: (sec, 0, b * per_seq + s, 0))

    grid_spec = pltpu.PrefetchScalarGridSpec(
        num_scalar_prefetch=1,
        grid=(batch, per_seq),
        in_specs=[
            sec_spec(0), sec_spec(1), sec_spec(2), sec_spec(3),
            pl.BlockSpec(lbs.shape, lambda b, s, l: (0, 0, 0)),
            pl.BlockSpec((None, 1, HEAD_DIM), lambda b, s, l: (l[0], 0, 0)),
            pl.BlockSpec(level_mat.shape, lambda b, s, l: (0, 0)),
        ],
        out_specs=pl.BlockSpec((n_heads, tile, LANES), lambda b, s, l: (0, b * per_seq + s, 0)),
        scratch_shapes=[pltpu.VMEM((n_heads, HEAD_DIM, HEAD_DIM), F32),
                        pltpu.VMEM((HGRN_CHUNK, HGRN_CHUNK), jnp.int32),
                        pltpu.VMEM((2, tile, HEAD_DIM), BF16),
                        pltpu.VMEM((2, tile, HEAD_DIM), BF16),
                        pltpu.VMEM((level_mat.shape[0], n_chunks * HEAD_DIM), F32),
                        pltpu.VMEM((level_mat.shape[0], n_chunks * HEAD_DIM), F32)],
    )
    return pl.pallas_call(
        _hgrn_kernel,
        grid_spec=grid_spec,
        out_shape=jax.ShapeDtypeStruct((n_heads, t, LANES), BF16),
        compiler_params=_params(2),
        name="hgrn2",
    )(l, proj5, proj5, proj5, proj5, lbs, norm_g, level_mat)


def _pool_kernel(l_ref, u_ref, pw_ref, ps_ref, o_ref, halo_ref):
    grp = pl.program_id(1)
    s = pl.program_id(2)
    tile = u_ref.shape[1]
    width = jnp.left_shift(2, grp)

    @pl.when(s == 0)
    def _():
        halo_ref[...] = jnp.zeros_like(halo_ref)

    u = jnp.concatenate([u_ref[0], u_ref[1]], axis=1)
    ext = jnp.concatenate([halo_ref[...], u], axis=0)

    trow = lax.broadcasted_iota(jnp.int32, (POOL_HALO, 2 * POOL_HALO), 0) + POOL_HALO
    rcol = lax.broadcasted_iota(jnp.int32, (POOL_HALO, 2 * POOL_HALO), 1)
    band = ((rcol <= trow) & (rcol > trow - width)).astype(BF16)
    sums = jnp.concatenate(
        [jnp.dot(band, ext[r * POOL_HALO:(r + 2) * POOL_HALO], preferred_element_type=F32)
         for r in range(tile // POOL_HALO)], axis=0)

    pos = lax.broadcasted_iota(jnp.int32, (tile, 1), 0) + s * tile + 1
    count = jnp.minimum(pos, width).astype(F32)
    pooled = sums / count - u.astype(F32)
    y = jnp.dot(pooled.astype(BF16), pw_ref[...], preferred_element_type=F32)
    o_ref[...] = (y * ps_ref[...]).astype(BF16)
    halo_ref[...] = u[tile - POOL_HALO:, :]


def _pool(l, proj, pool_w, pool_scale, batch, seq):
    t = proj.shape[1]
    depth, groups, gd, _ = pool_w.shape
    tile = POOL_TILE
    per_seq = seq // tile
    blocks_per_group = gd // LANES
    first_block = proj.shape[0] - groups * blocks_per_group
    grid_spec = pltpu.PrefetchScalarGridSpec(
        num_scalar_prefetch=1,
        grid=(batch, groups, per_seq),
        in_specs=[
            pl.BlockSpec((blocks_per_group, tile, LANES),
                         lambda b, g, s, l: (first_block // blocks_per_group + g, b * per_seq + s, 0)),
            pl.BlockSpec((None, None, gd, gd), lambda b, g, s, l: (l[0], g, 0, 0)),
            pl.BlockSpec((None, None, 1, gd), lambda b, g, s, l: (l[0], g, 0, 0)),
        ],
        out_specs=pl.BlockSpec((tile, gd), lambda b, g, s, l: (b * per_seq + s, g)),
        scratch_shapes=[pltpu.VMEM((POOL_HALO, gd), BF16)],
    )
    return pl.pallas_call(
        _pool_kernel,
        grid_spec=grid_spec,
        out_shape=jax.ShapeDtypeStruct((t, groups * gd), BF16),
        compiler_params=_params(3),
        name="pool_mix",
    )(l, proj, pool_w, pool_scale.reshape(depth, groups, 1, gd))


def _outproj_kernel(l_ref, ya_ref, yb_ref, w_ref, x_ref, gate_ref, g_ref, sh_ref, sc_ref, o_ref, h_ref):
    y = jnp.concatenate([ya_ref[h] for h in range(ya_ref.shape[0])] + [yb_ref[...]], axis=1)
    acc = jnp.dot(y, w_ref[...], preferred_element_type=F32)
    x1 = x_ref[...] + gate_ref[...] * acc
    o_ref[...] = x1
    h_ref[...] = _norm_mod(x1, g_ref[...], sc_ref[...], sh_ref[...]).astype(BF16)


def _out_proj(l, ya, yb, w_out, x2d, gains, mods, seq):
    t, d = x2d.shape
    n_heads = ya.shape[0]
    tm = OUTPROJ_TM
    per_seq = seq // tm

    def mod_spec(k):
        return pl.BlockSpec((None, None, None, 1, d), lambda i, l: (l[0], i // per_seq, k, 0, 0))

    grid_spec = pltpu.PrefetchScalarGridSpec(
        num_scalar_prefetch=1,
        grid=(t // tm,),
        in_specs=[
            pl.BlockSpec((n_heads, tm, LANES), lambda i, l: (0, i, 0)),
            pl.BlockSpec((tm, yb.shape[1]), lambda i, l: (i, 0)),
            pl.BlockSpec((None, d, d), lambda i, l: (l[0], 0, 0)),
            pl.BlockSpec((tm, d), lambda i, l: (i, 0)),
            mod_spec(2),
            pl.BlockSpec((None, 1, d), lambda i, l: (l[0], 0, 0)),
            mod_spec(3), mod_spec(4),
        ],
        out_specs=[pl.BlockSpec((tm, d), lambda i, l: (i, 0)),
                   pl.BlockSpec((tm, d), lambda i, l: (i, 0))],
    )
    return pl.pallas_call(
        _outproj_kernel,
        grid_spec=grid_spec,
        out_shape=[jax.ShapeDtypeStruct((t, d), F32), jax.ShapeDtypeStruct((t, d), BF16)],
        compiler_params=_params(1),
        name="out_proj",
    )(l, ya, yb, w_out, x2d, mods, gains, mods, mods)


def _shift_rows(a, k, halo):
    rolled = pltpu.roll(a, k, axis=0)
    prev = pltpu.roll(halo, k, axis=0)
    ri = lax.broadcasted_iota(jnp.int32, halo.shape, 0)
    top = jnp.where(ri < k, prev, rolled[0:halo.shape[0]])
    return jnp.concatenate([top, rolled[halo.shape[0]:]], axis=0)


def _ffn_kernel(l_ref, x_hbm, h_ref, gate_ref, fg_ref, wa_ref, wv_ref, cw_ref, cb_ref, wd_ref,
                o_hbm, acc_ref, halo_ref, res_ref, sem_ref, *, tiles_per_seq, depth):
    i = pl.program_id(0)
    j = pl.program_id(1)
    n_tiles = pl.num_programs(0)
    n_steps = pl.num_programs(1)
    halo_rows = halo_ref.shape[1]
    tm = acc_ref.shape[0]

    def fetch_rows(tile):
        return pltpu.make_async_copy(x_hbm.at[pl.ds(tile * tm, tm)], res_ref, sem_ref.at[0])

    def write_rows(tile):
        return pltpu.make_async_copy(res_ref, o_hbm.at[pl.ds(tile * tm, tm)], sem_ref.at[1])

    @pl.when(j == n_steps - 2)
    def _():
        @pl.when(i > 0)
        def _():
            write_rows(i - 1).wait()

        fetch_rows(i).start()

    @pl.when(j == 0)
    def _():
        acc_ref[...] = jnp.zeros_like(acc_ref)

    @pl.when(i % tiles_per_seq == 0)
    def _():
        halo_ref[j] = jnp.zeros(halo_ref.shape[1:], F32)

    h = h_ref[...]
    a = jnp.dot(h, wa_ref[...], preferred_element_type=F32)
    v = jnp.dot(h, wv_ref[...], preferred_element_type=F32)
    halo = halo_ref[j]
    cw = cw_ref[...]
    conv = cb_ref[...] + cw[CONV_WIDTH - 1:CONV_WIDTH] * a
    for back in range(1, CONV_WIDTH):
        tap = CONV_WIDTH - 1 - back
        conv = conv + cw[tap:tap + 1] * _shift_rows(a, back, halo)
    halo_ref[j] = a[a.shape[0] - halo_rows:, :]
    glu = (_silu(conv) * v).astype(BF16)
    acc_ref[...] += jnp.dot(glu, wd_ref[...], preferred_element_type=F32)

    @pl.when(j == n_steps - 1)
    def _():
        fetch_rows(i).wait()
        last_layer = l_ref[0] == depth - 1

        @pl.when(jnp.logical_not(last_layer))
        def _():
            res_ref[...] = res_ref[...] + gate_ref[...] * acc_ref[...]

        @pl.when(last_layer)
        def _():
            xf = res_ref[...] + gate_ref[...] * acc_ref[...]
            ms = jnp.mean(xf * xf, axis=-1, keepdims=True)
            res_ref[...] = xf * lax.rsqrt(ms + EPS) * fg_ref[...]

        write_rows(i).start()

        @pl.when(i == n_tiles - 1)
        def _():
            write_rows(i).wait()


def _ffn(l, x2d, h2d, mods, final_g, w_up, conv_w, conv_b, w_down, seq):
    t, d = x2d.shape
    depth, ff, _ = w_down.shape
    tm, tf = FFN_TM, FFN_TF
    per_seq = seq // tm
    n_ff = ff // tf

    grid_spec = pltpu.PrefetchScalarGridSpec(
        num_scalar_prefetch=1,
        grid=(t // tm, n_ff),
        in_specs=[
            pl.BlockSpec(memory_space=pl.ANY),
            pl.BlockSpec((tm, d), lambda i, j, l: (i, 0)),
            pl.BlockSpec((None, None, None, 1, d), lambda i, j, l: (l[0], i // per_seq, 5, 0, 0)),
            pl.BlockSpec((1, d), lambda i, j, l: (0, 0)),
            pl.BlockSpec((None, d, tf), lambda i, j, l: (l[0], 0, j)),
            pl.BlockSpec((None, d, tf), lambda i, j, l: (l[0], 0, n_ff + j)),
            pl.BlockSpec((None, CONV_WIDTH, tf), lambda i, j, l: (l[0], 0, j)),
            pl.BlockSpec((None, 1, tf), lambda i, j, l: (l[0], 0, j)),
            pl.BlockSpec((None, tf, d), lambda i, j, l: (l[0], j, 0)),
        ],
        out_specs=pl.BlockSpec(memory_space=pl.ANY),
        scratch_shapes=[
            pltpu.VMEM((tm, d), F32),
            pltpu.VMEM((n_ff, 8, tf), F32),
            pltpu.VMEM((tm, d), F32),
            pltpu.SemaphoreType.DMA((2,)),
        ],
    )
    return pl.pallas_call(
        functools.partial(_ffn_kernel, tiles_per_seq=per_seq, depth=depth),
        grid_spec=grid_spec,
        out_shape=jax.ShapeDtypeStruct((t, d), F32),
        compiler_params=_params(2),
        name="conv_glu_ffn",
    )(l, x2d, h2d, mods, final_g, w_up, w_up, conv_w, conv_b.reshape(depth, 1, ff), w_down)


def kernel(x, c, ada_w, ada_b, mix_norm_g, w_in, hgrn_lower_bounds, hgrn_norm_g, pool_w, pool_scale,
           w_out, ffn_norm_g, w_up, conv_w, conv_b, w_down, final_norm_g):
    batch, seq, d = x.shape
    depth = w_in.shape[0]
    d_hgrn = hgrn_lower_bounds.shape[1]
    n_heads = d_hgrn // HEAD_DIM

    mods = _modulation(c, ada_w, ada_b).reshape(depth, batch, N_MOD, 1, d)
    w_in_b, w_out_b = w_in.astype(BF16), w_out.astype(BF16)
    w_up_b, w_down_b, pool_w_b = w_up.astype(BF16), w_down.astype(BF16), pool_w.astype(BF16)
    lbs = hgrn_lower_bounds.reshape(depth, n_heads, HEAD_DIM).transpose(1, 0, 2)
    mix_g = mix_norm_g.reshape(depth, 1, d)
    ffn_g = ffn_norm_g.reshape(depth, 1, d)
    head_g = hgrn_norm_g.reshape(depth, 1, HEAD_DIM)
    final_g = final_norm_g.reshape(1, d)
    level_mat = jnp.asarray(_hgrn_level_matrix(HGRN_CHUNK), dtype=BF16)

    xc = x.reshape(batch * seq, d)
    for li in range(depth):
        l = jnp.full((1,), li, jnp.int32)
        proj = _in_proj(l, xc, mix_g, mods, w_in_b, seq)
        ya = _hgrn(l, proj, lbs, head_g, level_mat, batch, seq)
        yb = _pool(l, proj, pool_w_b, pool_scale, batch, seq)
        x1, h2 = _out_proj(l, ya, yb, w_out_b, xc, ffn_g, mods, seq)
        xc = _ffn(l, x1, h2, mods, final_g, w_up_b, conv_w, conv_b, w_down_b, seq)
    return xc.reshape(batch, seq, d)
```

```python
import functools

import jax
import jax.numpy as jnp
import numpy as np
from jax import lax
from jax.experimental import pallas as pl
from jax.experimental.pallas import tpu as pltpu

F32 = jnp.float32
BF16 = jnp.bfloat16

LANES = 128
HEAD_DIM = 128
POOL_WINDOWS = (2, 4, 8, 16)
POOL_HALO = 128
CONV_WIDTH = 3
N_MOD = 6
EPS = 1e-6
LOG2E = 1.4426950408889634
VMEM_LIMIT = 56 * 1024 * 1024

HGRN_CHUNK = 128
HGRN_TILE = 1024
POOL_TILE = 1024
INPROJ_TM, INPROJ_TN = 512, 1024
OUTPROJ_TM = 512
FFN_TM, FFN_TF = 1024, 512
MOD_TN = 1024


def _params(n_axes, vmem=VMEM_LIMIT):
    return pltpu.CompilerParams(
        dimension_semantics=("arbitrary",) * n_axes, vmem_limit_bytes=vmem)


def _norm_mod(xf, gain, scale, shift):
    ms = jnp.mean(xf * xf, axis=-1, keepdims=True)
    y = xf * lax.rsqrt(ms + EPS) * gain
    return y * (1.0 + scale) + shift


def _silu(x):
    return (0.5 * x) * (1.0 + jnp.tanh(0.5 * x))


def _mod_kernel(c_ref, w_ref, b_ref, o_ref):
    c = c_ref[...]
    c_act = _silu(c).astype(BF16)
    acc = jnp.dot(c_act, w_ref[...].astype(BF16), preferred_element_type=F32)
    o_ref[...] = acc + b_ref[...]


def _modulation(c, ada_w, ada_b):
    depth, d, n = ada_w.shape
    b = c.shape[0]
    return pl.pallas_call(
        _mod_kernel,
        grid=(depth, n // MOD_TN),
        in_specs=[
            pl.BlockSpec((b, d), lambda l, j: (0, 0)),
            pl.BlockSpec((None, d, MOD_TN), lambda l, j: (l, 0, j)),
            pl.BlockSpec((None, 1, MOD_TN), lambda l, j: (l, 0, j)),
        ],
        out_specs=pl.BlockSpec((None, b, MOD_TN), lambda l, j: (l, 0, j)),
        out_shape=jax.ShapeDtypeStruct((depth, b, n), F32),
        compiler_params=_params(2),
        name="ada_mod",
    )(c, ada_w, ada_b.reshape(depth, 1, n))


def _inproj_kernel(l_ref, x_ref, g_ref, sh_ref, sc_ref, w_ref, o_ref):
    h = _norm_mod(x_ref[...], g_ref[...], sc_ref[...], sh_ref[...]).astype(BF16)
    blocks_per_dot = INPROJ_TN // LANES
    for j in range(0, o_ref.shape[0], blocks_per_dot):
        acc = jnp.dot(h, w_ref[:, j * LANES:(j + blocks_per_dot) * LANES], preferred_element_type=F32)
        for blk in range(blocks_per_dot):
            o_ref[j + blk] = acc[:, blk * LANES:(blk + 1) * LANES].astype(BF16)


def _in_proj(l, x2d, gains, mods, w_in, seq):
    t, d = x2d.shape
    n = w_in.shape[2]
    tm = INPROJ_TM
    per_seq = seq // tm
    grid_spec = pltpu.PrefetchScalarGridSpec(
        num_scalar_prefetch=1,
        grid=(t // tm,),
        in_specs=[
            pl.BlockSpec((tm, d), lambda i, l: (i, 0)),
            pl.BlockSpec((None, 1, d), lambda i, l: (l[0], 0, 0)),
            pl.BlockSpec((None, None, None, 1, d), lambda i, l: (l[0], i // per_seq, 0, 0, 0)),
            pl.BlockSpec((None, None, None, 1, d), lambda i, l: (l[0], i // per_seq, 1, 0, 0)),
            pl.BlockSpec((None, d, n), lambda i, l: (l[0], 0, 0), pipeline_mode=pl.Buffered(1)),
        ],
        out_specs=pl.BlockSpec((n // LANES, tm, LANES), lambda i, l: (0, i, 0)),
    )
    return pl.pallas_call(
        _inproj_kernel,
        grid_spec=grid_spec,
        out_shape=jax.ShapeDtypeStruct((n // LANES, t, LANES), BF16),
        compiler_params=_params(1),
        name="in_proj",
    )(l, x2d, gains, mods, mods, w_in)


def _hgrn_level_matrix(chunk):
    t = np.arange(chunk)[:, None]
    r = np.arange(chunk)[None, :]
    low = (r <= t).astype(np.float32)
    mats = [low]
    m = 1
    while m < chunk:
        anchor = (t // (2 * m)) * (2 * m) + m - 1
        sign = np.where(t > anchor, 1.0, -1.0)
        mats.append(sign * (low - (r <= anchor).astype(np.float32)))
        m *= 2
    stacked = np.concatenate(mats, axis=0)
    return np.concatenate([stacked] * 2, axis=1)


_NT = (((1,), (1,)), ((), ()))
_TN = (((0,), (0,)), ((), ()))


def _hgrn_kernel(l_ref, q_ref, f_ref, i_ref, g_ref, lb_ref, ng_ref, lm_ref, o_ref,
                 st_ref, lvl_ref, qk0_ref, qk1_ref, bd0_ref, bd1_ref):
    chunk = HGRN_CHUNK
    n_levels = chunk.bit_length() - 1
    n_heads, tile, _ = q_ref.shape
    n_chunks = tile // chunk
    layer = l_ref[0]

    @pl.when((pl.program_id(0) == 0) & (pl.program_id(1) == 0))
    def _():
        trow = lax.broadcasted_iota(jnp.int32, (chunk, chunk), 0)
        tcol = lax.broadcasted_iota(jnp.int32, (chunk, chunk), 1)
        txor = trow ^ tcol
        top = jnp.full((chunk, chunk), -1, jnp.int32)
        for k in range(n_levels):
            top = top + ((txor >> k) != 0).astype(jnp.int32)
        lvl_ref[...] = jnp.where(trow > tcol, top, jnp.where(trow == tcol, n_levels, -1))

    @pl.when(pl.program_id(1) == 0)
    def _():
        st_ref[...] = jnp.zeros_like(st_ref)

    gain = ng_ref[...]

    def gates(h, qk_ref, bd_ref):
        lbraw = lb_ref[h]
        ex = jnp.exp(lbraw - jnp.max(lbraw, axis=0, keepdims=True))
        p = ex / jnp.sum(ex, axis=0, keepdims=True)
        lrow = lax.broadcasted_iota(jnp.int32, p.shape, 0)
        cum = jnp.sum(jnp.where(lrow <= layer, p, 0.0), axis=0, keepdims=True)
        lb = jnp.clip(cum - p[0:1], 0.0, 1.0)
        half_key = 0.5 * (1.0 - lb)

        parts = []
        for c in range(n_chunks):
            rows = pl.ds(c * chunk, chunk)
            z = f_ref[h, rows, :].astype(F32)
            e = jnp.exp2(z * (-LOG2E))
            logf = (jnp.log(1.0 + lb * e) - jnp.log(1.0 + e)) * LOG2E
            key = half_key - half_key * jnp.tanh(0.5 * z)
            qraw = q_ref[h, rows, :].astype(F32)
            qq = (qraw * (0.5 * HEAD_DIM ** -0.5)) * (1.0 + jnp.tanh(0.5 * qraw))
            qk_ref[0, rows, :] = qq.astype(BF16)
            qk_ref[1, rows, :] = key.astype(BF16)
            hi = logf.astype(BF16)
            lo = (logf - hi.astype(F32)).astype(BF16)
            parts.append(jnp.concatenate([hi, lo], axis=0))

        bd = jnp.dot(lm_ref[...], jnp.concatenate(parts, axis=1), preferred_element_type=F32)
        for c in range(n_chunks):
            bd_ref[c] = bd[:, c * HEAD_DIM:(c + 1) * HEAD_DIM]

    def mix(h, qk_ref, bd_ref):
        st = st_ref[h]
        lvl = lvl_ref[...]
        all_scores = []
        for c in range(n_chunks):
            rows = pl.ds(c * chunk, chunk)
            s0 = lax.dot_general(qk_ref[0, rows, :], qk_ref[1, rows, :], _NT, preferred_element_type=F32)
            all_scores.append(jnp.where(lvl == n_levels, s0, 0.0))
        for k in range(n_levels):
            for c in range(n_chunks):
                rows = pl.ds(c * chunk, chunk)
                w = jnp.exp2(bd_ref[c, pl.ds(chunk * (k + 1), chunk), :]).astype(BF16)
                sk = lax.dot_general(qk_ref[0, rows, :] * w, qk_ref[1, rows, :] * w, _NT,
                                     preferred_element_type=F32)
                all_scores[c] = jnp.where(lvl == k, sk, all_scores[c])
        all_scores = [s.astype(BF16) for s in all_scores]

        for c in range(n_chunks):
            rows = pl.ds(c * chunk, chunk)
            qq = qk_ref[0, rows, :]
            key = qk_ref[1, rows, :]
            val = i_ref[h, rows, :]
            b = bd_ref[c, pl.ds(0, chunk), :]
            o = jnp.dot(all_scores[c], val, preferred_element_type=F32)
            o = o + lax.dot_general(qq * jnp.exp2(b).astype(BF16), st.astype(BF16), _NT,
                                    preferred_element_type=F32)

            b_last = b[chunk - 1:chunk]
            k_out = key * jnp.exp2(b_last - b).astype(BF16)
            st = st * jnp.exp2(b_last) + lax.dot_general(val, k_out, _TN, preferred_element_type=F32)

            ms = jnp.mean(o * o, axis=-1, keepdims=True)
            on = o * lax.rsqrt(ms + EPS) * gain
            graw = g_ref[h, rows, :].astype(F32)
            o_ref[h, rows, :] = (on * _silu(graw)).astype(BF16)
        st_ref[h] = st

    gates(0, qk0_ref, bd0_ref)

    def pair_body(j, carry):
        h0 = 2 * j
        mix(h0, qk0_ref, bd0_ref)
        gates(h0 + 1, qk1_ref, bd1_ref)
        mix(h0 + 1, qk1_ref, bd1_ref)
        gates(jnp.minimum(h0 + 2, n_heads - 1), qk0_ref, bd0_ref)
        return carry

    lax.fori_loop(0, n_heads // 2, pair_body, 0)


def _hgrn(l, proj, lbs, norm_g, level_mat, batch, seq):
    n_heads = lbs.shape[0]
    t = proj.shape[1]
    tile = HGRN_TILE
    per_seq = seq // tile
    n_chunks = tile // HGRN_CHUNK
    proj5 = proj.reshape(5, n_heads, t, LANES)

    def sec_spec(sec):
        return pl.BlockSpec((None, n_heads, tile, LANES),
                            lambda b, s, l: (sec, 0, b * per_seq + s, 0))

    grid_spec = pltpu.PrefetchScalarGridSpec(
        num_scalar_prefetch=1,
        grid=(batch, per_seq),
        in_specs=[
            sec_spec(0), sec_spec(1), sec_spec(2), sec_spec(3),
            pl.BlockSpec(lbs.shape, lambda b, s, l: (0, 0, 0)),
            pl.BlockSpec((None, 1, HEAD_DIM), lambda b, s, l: (l[0], 0, 0)),
            pl.BlockSpec(level_mat.shape, lambda b, s, l: (0, 0)),
        ],
        out_specs=pl.BlockSpec((n_heads, tile, LANES), lambda b, s, l: (0, b * per_seq + s, 0)),
        scratch_shapes=[pltpu.VMEM((n_heads, HEAD_DIM, HEAD_DIM), F32),
                        pltpu.VMEM((HGRN_CHUNK, HGRN_CHUNK), jnp.int32),
                        pltpu.VMEM((2, tile, HEAD_DIM), BF16),
                        pltpu.VMEM((2, tile, HEAD_DIM), BF16),
                        pltpu.VMEM((n_chunks, level_mat.shape[0], HEAD_DIM), F32),
                        pltpu.VMEM((n_chunks, level_mat.shape[0], HEAD_DIM), F32)],
    )
    return pl.pallas_call(
        _hgrn_kernel,
        grid_spec=grid_spec,
        out_shape=jax.ShapeDtypeStruct((n_heads, t, LANES), BF16),
        compiler_params=_params(2),
        name="hgrn2",
    )(l, proj5, proj5, proj5, proj5, lbs, norm_g, level_mat)


def _pool_kernel(l_ref, u_ref, pw_ref, ps_ref, o_ref, halo_ref):
    grp = pl.program_id(1)
    s = pl.program_id(2)
    tile = u_ref.shape[1]
    width = jnp.left_shift(2, grp)

    @pl.when(s == 0)
    def _():
        halo_ref[...] = jnp.zeros_like(halo_ref)

    u = jnp.concatenate([u_ref[0], u_ref[1]], axis=1)
    ext = jnp.concatenate([halo_ref[...], u], axis=0)

    trow = lax.broadcasted_iota(jnp.int32, (POOL_HALO, 2 * POOL_HALO), 0) + POOL_HALO
    rcol = lax.broadcasted_iota(jnp.int32, (POOL_HALO, 2 * POOL_HALO), 1)
    band = ((rcol <= trow) & (rcol > trow - width)).astype(BF16)
    sums = jnp.concatenate(
        [jnp.dot(band, ext[r * POOL_HALO:(r + 2) * POOL_HALO], preferred_element_type=F32)
         for r in range(tile // POOL_HALO)], axis=0)

    pos = lax.broadcasted_iota(jnp.int32, (tile, 1), 0) + s * tile + 1
    count = jnp.minimum(pos, width).astype(F32)
    pooled = sums / count - u.astype(F32)
    y = jnp.dot(pooled.astype(BF16), pw_ref[...], preferred_element_type=F32)
    o_ref[...] = (y * ps_ref[...]).astype(BF16)
    halo_ref[...] = u[tile - POOL_HALO:, :]


def _pool(l, proj, pool_w, pool_scale, batch, seq):
    t = proj.shape[1]
    depth, groups, gd, _ = pool_w.shape
    tile = POOL_TILE
    per_seq = seq // tile
    blocks_per_group = gd // LANES
    first_block = proj.shape[0] - groups * blocks_per_group
    grid_spec = pltpu.PrefetchScalarGridSpec(
        num_scalar_prefetch=1,
        grid=(batch, groups, per_seq),
        in_specs=[
            pl.BlockSpec((blocks_per_group, tile, LANES),
                         lambda b, g, s, l: (first_block // blocks_per_group + g, b * per_seq + s, 0)),
            pl.BlockSpec((None, None, gd, gd), lambda b, g, s, l: (l[0], g, 0, 0)),
            pl.BlockSpec((None, None, 1, gd), lambda b, g, s, l: (l[0], g, 0, 0)),
        ],
        out_specs=pl.BlockSpec((tile, gd), lambda b, g, s, l: (b * per_seq + s, g)),
        scratch_shapes=[pltpu.VMEM((POOL_HALO, gd), BF16)],
    )
    return pl.pallas_call(
        _pool_kernel,
        grid_spec=grid_spec,
        out_shape=jax.ShapeDtypeStruct((t, groups * gd), BF16),
        compiler_params=_params(3),
        name="pool_mix",
    )(l, proj, pool_w, pool_scale.reshape(depth, groups, 1, gd))


def _outproj_kernel(l_ref, ya_ref, yb_ref, w_ref, x_ref, gate_ref, g_ref, sh_ref, sc_ref, o_ref, h_ref):
    y = jnp.concatenate([ya_ref[h] for h in range(ya_ref.shape[0])] + [yb_ref[...]], axis=1)
    acc = jnp.dot(y, w_ref[...], preferred_element_type=F32)
    x1 = x_ref[...] + gate_ref[...] * acc
    o_ref[...] = x1
    h_ref[...] = _norm_mod(x1, g_ref[...], sc_ref[...], sh_ref[...]).astype(BF16)


def _out_proj(l, ya, yb, w_out, x2d, gains, mods, seq):
    t, d = x2d.shape
    n_heads = ya.shape[0]
    tm = OUTPROJ_TM
    per_seq = seq // tm

    def mod_spec(k):
        return pl.BlockSpec((None, None, None, 1, d), lambda i, l: (l[0], i // per_seq, k, 0, 0))

    grid_spec = pltpu.PrefetchScalarGridSpec(
        num_scalar_prefetch=1,
        grid=(t // tm,),
        in_specs=[
            pl.BlockSpec((n_heads, tm, LANES), lambda i, l: (0, i, 0)),
            pl.BlockSpec((tm, yb.shape[1]), lambda i, l: (i, 0)),
            pl.BlockSpec((None, d, d), lambda i, l: (l[0], 0, 0)),
            pl.BlockSpec((tm, d), lambda i, l: (i, 0)),
            mod_spec(2),
            pl.BlockSpec((None, 1, d), lambda i, l: (l[0], 0, 0)),
            mod_spec(3), mod_spec(4),
        ],
        out_specs=[pl.BlockSpec((tm, d), lambda i, l: (i, 0)),
                   pl.BlockSpec((tm, d), lambda i, l: (i, 0))],
    )
    return pl.pallas_call(
        _outproj_kernel,
        grid_spec=grid_spec,
        out_shape=[jax.ShapeDtypeStruct((t, d), F32), jax.ShapeDtypeStruct((t, d), BF16)],
        compiler_params=_params(1),
        name="out_proj",
    )(l, ya, yb, w_out, x2d, mods, gains, mods, mods)


def _shift_rows(a, k, halo):
    rolled = pltpu.roll(a, k, axis=0)
    prev = pltpu.roll(halo, k, axis=0)
    ri = lax.broadcasted_iota(jnp.int32, halo.shape, 0)
    top = jnp.where(ri < k, prev, rolled[0:halo.shape[0]])
    return jnp.concatenate([top, rolled[halo.shape[0]:]], axis=0)


def _ffn_kernel(l_ref, x_hbm, h_ref, gate_ref, fg_ref, wa_ref, wv_ref, cw_ref, cb_ref, wd_ref,
                o_hbm, acc_ref, halo_ref, res_ref, sem_ref, *, tiles_per_seq, depth):
    i = pl.program_id(0)
    j = pl.program_id(1)
    n_tiles = pl.num_programs(0)
    n_steps = pl.num_programs(1)
    halo_rows = halo_ref.shape[1]
    tm = acc_ref.shape[0]

    def fetch_rows(tile):
        return pltpu.make_async_copy(x_hbm.at[pl.ds(tile * tm, tm)], res_ref, sem_ref.at[0])

    def write_rows(tile):
        return pltpu.make_async_copy(res_ref, o_hbm.at[pl.ds(tile * tm, tm)], sem_ref.at[1])

    @pl.when(j == n_steps - 2)
    def _():
        @pl.when(i > 0)
        def _():
            write_rows(i - 1).wait()

        fetch_rows(i).start()

    @pl.when(j == 0)
    def _():
        acc_ref[...] = jnp.zeros_like(acc_ref)

    @pl.when(i % tiles_per_seq == 0)
    def _():
        halo_ref[j] = jnp.zeros(halo_ref.shape[1:], F32)

    h = h_ref[...]
    a = jnp.dot(h, wa_ref[...], preferred_element_type=F32)
    v = jnp.dot(h, wv_ref[...], preferred_element_type=F32)
    halo = halo_ref[j]
    cw = cw_ref[...]
    conv = cb_ref[...] + cw[CONV_WIDTH - 1:CONV_WIDTH] * a
    for back in range(1, CONV_WIDTH):
        tap = CONV_WIDTH - 1 - back
        conv = conv + cw[tap:tap + 1] * _shift_rows(a, back, halo)
    halo_ref[j] = a[a.shape[0] - halo_rows:, :]
    glu = (_silu(conv) * v).astype(BF16)
    acc_ref[...] += jnp.dot(glu, wd_ref[...].astype(BF16), preferred_element_type=F32)

    @pl.when(j == n_steps - 1)
    def _():
        fetch_rows(i).wait()
        last_layer = l_ref[0] == depth - 1

        @pl.when(jnp.logical_not(last_layer))
        def _():
            res_ref[...] = res_ref[...] + gate_ref[...] * acc_ref[...]

        @pl.when(last_layer)
        def _():
            xf = res_ref[...] + gate_ref[...] * acc_ref[...]
            ms = jnp.mean(xf * xf, axis=-1, keepdims=True)
            res_ref[...] = xf * lax.rsqrt(ms + EPS) * fg_ref[...]

        write_rows(i).start()

        @pl.when(i == n_tiles - 1)
        def _():
            write_rows(i).wait()


def _ffn(l, x2d, h2d, mods, final_g, w_up, conv_w, conv_b, w_down, seq):
    t, d = x2d.shape
    depth, ff, _ = w_down.shape
    tm, tf = FFN_TM, FFN_TF
    per_seq = seq // tm
    n_ff = ff // tf

    grid_spec = pltpu.PrefetchScalarGridSpec(
        num_scalar_prefetch=1,
        grid=(t // tm, n_ff),
        in_specs=[
            pl.BlockSpec(memory_space=pl.ANY),
            pl.BlockSpec((tm, d), lambda i, j, l: (i, 0)),
            pl.BlockSpec((None, None, None, 1, d), lambda i, j, l: (l[0], i // per_seq, 5, 0, 0)),
            pl.BlockSpec((1, d), lambda i, j, l: (0, 0)),
            pl.BlockSpec((None, d, tf), lambda i, j, l: (l[0], 0, j)),
            pl.BlockSpec((None, d, tf), lambda i, j, l: (l[0], 0, n_ff + j)),
            pl.BlockSpec((None, CONV_WIDTH, tf), lambda i, j, l: (l[0], 0, j)),
            pl.BlockSpec((None, 1, tf), lambda i, j, l: (l[0], 0, j)),
            pl.BlockSpec((None, tf, d), lambda i, j, l: (l[0], j, 0)),
        ],
        out_specs=pl.BlockSpec(memory_space=pl.ANY),
        scratch_shapes=[
            pltpu.VMEM((tm, d), F32),
            pltpu.VMEM((n_ff, 8, tf), F32),
            pltpu.VMEM((tm, d), F32),
            pltpu.SemaphoreType.DMA((2,)),
        ],
    )
    return pl.pallas_call(
        functools.partial(_ffn_kernel, tiles_per_seq=per_seq, depth=depth),
        grid_spec=grid_spec,
        out_shape=jax.ShapeDtypeStruct((t, d), F32),
        compiler_params=_params(2),
        name="conv_glu_ffn",
    )(l, x2d, h2d, mods, final_g, w_up, w_up, conv_w, conv_b.reshape(depth, 1, ff), w_down)


def kernel(x, c, ada_w, ada_b, mix_norm_g, w_in, hgrn_lower_bounds, hgrn_norm_g, pool_w, pool_scale,
           w_out, ffn_norm_g, w_up, conv_w, conv_b, w_down, final_norm_g):
    batch, seq, d = x.shape
    depth = w_in.shape[0]
    d_hgrn = hgrn_lower_bounds.shape[1]
    n_heads = d_hgrn // HEAD_DIM

    mods = _modulation(c, ada_w, ada_b).reshape(depth, batch, N_MOD, 1, d)
    w_in_b, w_out_b = w_in.astype(BF16), w_out.astype(BF16)
    w_up_b, pool_w_b = w_up.astype(BF16), pool_w.astype(BF16)
    lbs = hgrn_lower_bounds.reshape(depth, n_heads, HEAD_DIM).transpose(1, 0, 2)
    mix_g = mix_norm_g.reshape(depth, 1, d)
    ffn_g = ffn_norm_g.reshape(depth, 1, d)
    head_g = hgrn_norm_g.reshape(depth, 1, HEAD_DIM)
    final_g = final_norm_g.reshape(1, d)
    level_mat = jnp.asarray(_hgrn_level_matrix(HGRN_CHUNK), dtype=BF16)

    xc = x.reshape(batch * seq, d)
    for li in range(depth):
        l = jnp.full((1,), li, jnp.int32)
        proj = _in_proj(l, xc, mix_g, mods, w_in_b, seq)
        ya = _hgrn(l, proj, lbs, head_g, level_mat, batch, seq)
        yb = _pool(l, proj, pool_w_b, pool_scale, batch, seq)
        x1, h2 = _out_proj(l, ya, yb, w_out_b, xc, ffn_g, mods, seq)
        xc = _ffn(l, x1, h2, mods, final_g, w_up_b, conv_w, conv_b, w_down, seq)
    return xc.reshape(batch, seq, d)
```

```python
import functools

import jax
import jax.numpy as jnp
import numpy as np
from jax import lax
from jax.experimental import pallas as pl
from jax.experimental.pallas import tpu as pltpu

F32 = jnp.float32
BF16 = jnp.bfloat16

LANES = 128
HEAD_DIM = 128
POOL_WINDOWS = (2, 4, 8, 16)
POOL_HALO = 128
CONV_WIDTH = 3
N_MOD = 6
EPS = 1e-6
LOG2E = 1.4426950408889634
VMEM_LIMIT = 56 * 1024 * 1024

HGRN_CHUNK = 128
HGRN_TILE = 1024
INPROJ_TM, INPROJ_TN = 512, 1024
OUTPROJ_TM = 512
FFN_TM, FFN_TF = 1024, 512
MOD_TN = 1024


def _params(n_axes, vmem=VMEM_LIMIT):
    return pltpu.CompilerParams(
        dimension_semantics=("arbitrary",) * n_axes, vmem_limit_bytes=vmem)


def _norm_mod(xf, gain, scale, shift):
    ms = jnp.mean(xf * xf, axis=-1, keepdims=True)
    y = xf * lax.rsqrt(ms + EPS) * gain
    return y * (1.0 + scale) + shift


def _silu(x):
    return (0.5 * x) * (1.0 + jnp.tanh(0.5 * x))


def _mod_kernel(c_ref, w_ref, b_ref, o_ref):
    c = c_ref[...]
    c_act = _silu(c).astype(BF16)
    acc = jnp.dot(c_act, w_ref[...].astype(BF16), preferred_element_type=F32)
    o_ref[...] = acc + b_ref[...]


def _modulation(c, ada_w, ada_b):
    depth, d, n = ada_w.shape
    b = c.shape[0]
    return pl.pallas_call(
        _mod_kernel,
        grid=(depth, n // MOD_TN),
        in_specs=[
            pl.BlockSpec((b, d), lambda l, j: (0, 0)),
            pl.BlockSpec((None, d, MOD_TN), lambda l, j: (l, 0, j)),
            pl.BlockSpec((None, 1, MOD_TN), lambda l, j: (l, 0, j)),
        ],
        out_specs=pl.BlockSpec((None, b, MOD_TN), lambda l, j: (l, 0, j)),
        out_shape=jax.ShapeDtypeStruct((depth, b, n), F32),
        compiler_params=_params(2),
        name="ada_mod",
    )(c, ada_w, ada_b.reshape(depth, 1, n))


def _inproj_kernel(l_ref, x_ref, g_ref, sh_ref, sc_ref, w_ref, o_ref):
    h = _norm_mod(x_ref[...], g_ref[...], sc_ref[...], sh_ref[...]).astype(BF16)
    blocks_per_dot = INPROJ_TN // LANES
    for j in range(0, o_ref.shape[0], blocks_per_dot):
        acc = jnp.dot(h, w_ref[:, j * LANES:(j + blocks_per_dot) * LANES], preferred_element_type=F32)
        for blk in range(blocks_per_dot):
            o_ref[j + blk] = acc[:, blk * LANES:(blk + 1) * LANES].astype(BF16)


def _in_proj(l, x2d, gains, mods, w_in, seq):
    t, d = x2d.shape
    n = w_in.shape[2]
    tm = INPROJ_TM
    per_seq = seq // tm
    grid_spec = pltpu.PrefetchScalarGridSpec(
        num_scalar_prefetch=1,
        grid=(t // tm,),
        in_specs=[
            pl.BlockSpec((tm, d), lambda i, l: (i, 0)),
            pl.BlockSpec((None, 1, d), lambda i, l: (l[0], 0, 0)),
            pl.BlockSpec((None, None, None, 1, d), lambda i, l: (l[0], i // per_seq, 0, 0, 0)),
            pl.BlockSpec((None, None, None, 1, d), lambda i, l: (l[0], i // per_seq, 1, 0, 0)),
            pl.BlockSpec((None, d, n), lambda i, l: (l[0], 0, 0), pipeline_mode=pl.Buffered(1)),
        ],
        out_specs=pl.BlockSpec((n // LANES, tm, LANES), lambda i, l: (0, i, 0)),
    )
    return pl.pallas_call(
        _inproj_kernel,
        grid_spec=grid_spec,
        out_shape=jax.ShapeDtypeStruct((n // LANES, t, LANES), BF16),
        compiler_params=_params(1),
        name="in_proj",
    )(l, x2d, gains, mods, mods, w_in)


def _hgrn_level_matrix(chunk):
    t = np.arange(chunk)[:, None]
    r = np.arange(chunk)[None, :]
    low = (r <= t).astype(np.float32)
    mats = [low]
    m = 1
    while m < chunk:
        anchor = (t // (2 * m)) * (2 * m) + m - 1
        sign = np.where(t > anchor, 1.0, -1.0)
        mats.append(sign * (low - (r <= anchor).astype(np.float32)))
        m *= 2
    stacked = np.concatenate(mats, axis=0)
    return np.concatenate([stacked] * 2, axis=1)


_NT = (((1,), (1,)), ((), ()))
_TN = (((0,), (0,)), ((), ()))


def _hgrn_kernel(l_ref, q_ref, f_ref, i_ref, g_ref, lb_ref, ng_ref, lm_ref, o_ref,
                 st_ref, lvl_ref, qk0_ref, qk1_ref, bd0_ref, bd1_ref):
    chunk = HGRN_CHUNK
    n_levels = chunk.bit_length() - 1
    n_heads, tile, _ = q_ref.shape
    n_chunks = tile // chunk
    layer = l_ref[0]

    @pl.when((pl.program_id(0) == 0) & (pl.program_id(1) == 0))
    def _():
        trow = lax.broadcasted_iota(jnp.int32, (chunk, chunk), 0)
        tcol = lax.broadcasted_iota(jnp.int32, (chunk, chunk), 1)
        txor = trow ^ tcol
        top = jnp.full((chunk, chunk), -1, jnp.int32)
        for k in range(n_levels):
            top = top + ((txor >> k) != 0).astype(jnp.int32)
        lvl_ref[...] = jnp.where(trow > tcol, top, jnp.where(trow == tcol, n_levels, -1))

    @pl.when(pl.program_id(1) == 0)
    def _():
        st_ref[...] = jnp.zeros_like(st_ref)

    gain = ng_ref[...]

    def gates(h, qk_ref, bd_ref):
        lbraw = lb_ref[h]
        ex = jnp.exp(lbraw - jnp.max(lbraw, axis=0, keepdims=True))
        p = ex / jnp.sum(ex, axis=0, keepdims=True)
        lrow = lax.broadcasted_iota(jnp.int32, p.shape, 0)
        cum = jnp.sum(jnp.where(lrow <= layer, p, 0.0), axis=0, keepdims=True)
        lb = jnp.clip(cum - p[0:1], 0.0, 1.0)
        half_key = 0.5 * (1.0 - lb)

        parts = []
        for c in range(n_chunks):
            rows = pl.ds(c * chunk, chunk)
            z = f_ref[h, rows, :].astype(F32)
            e = jnp.exp2(z * (-LOG2E))
            logf = (jnp.log(1.0 + lb * e) - jnp.log(1.0 + e)) * LOG2E
            key = half_key - half_key * jnp.tanh(0.5 * z)
            qraw = q_ref[h, rows, :].astype(F32)
            qq = (qraw * (0.5 * HEAD_DIM ** -0.5)) * (1.0 + jnp.tanh(0.5 * qraw))
            qk_ref[0, rows, :] = qq.astype(BF16)
            qk_ref[1, rows, :] = key.astype(BF16)
            hi = logf.astype(BF16)
            lo = (logf - hi.astype(F32)).astype(BF16)
            parts.append(jnp.concatenate([hi, lo], axis=0))

        bd = jnp.dot(lm_ref[...], jnp.concatenate(parts, axis=1), preferred_element_type=F32)
        for c in range(n_chunks):
            bd_ref[c] = bd[:, c * HEAD_DIM:(c + 1) * HEAD_DIM]

    def mix(h, qk_ref, bd_ref):
        st = st_ref[h]
        lvl = lvl_ref[...]
        all_scores = []
        for c in range(n_chunks):
            rows = pl.ds(c * chunk, chunk)
            s0 = lax.dot_general(qk_ref[0, rows, :], qk_ref[1, rows, :], _NT, preferred_element_type=F32)
            all_scores.append(jnp.where(lvl == n_levels, s0, 0.0))
        for k in range(n_levels):
            for c in range(n_chunks):
                rows = pl.ds(c * chunk, chunk)
                w = jnp.exp2(bd_ref[c, pl.ds(chunk * (k + 1), chunk), :]).astype(BF16)
                sk = lax.dot_general(qk_ref[0, rows, :] * w, qk_ref[1, rows, :] * w, _NT,
                                     preferred_element_type=F32)
                all_scores[c] = jnp.where(lvl == k, sk, all_scores[c])
        all_scores = [s.astype(BF16) for s in all_scores]

        for c in range(n_chunks):
            rows = pl.ds(c * chunk, chunk)
            qq = qk_ref[0, rows, :]
            key = qk_ref[1, rows, :]
            val = i_ref[h, rows, :]
            b = bd_ref[c, pl.ds(0, chunk), :]
            o = jnp.dot(all_scores[c], val, preferred_element_type=F32)
            o = o + lax.dot_general(qq * jnp.exp2(b).astype(BF16), st.astype(BF16), _NT,
                                    preferred_element_type=F32)

            b_last = b[chunk - 1:chunk]
            k_out = key * jnp.exp2(b_last - b).astype(BF16)
            st = st * jnp.exp2(b_last) + lax.dot_general(val, k_out, _TN, preferred_element_type=F32)

            ms = jnp.mean(o * o, axis=-1, keepdims=True)
            on = o * lax.rsqrt(ms + EPS) * gain
            graw = g_ref[h, rows, :].astype(F32)
            o_ref[h, rows, :] = (on * _silu(graw)).astype(BF16)
        st_ref[h] = st

    gates(0, qk0_ref, bd0_ref)

    def pair_body(j, carry):
        h0 = 2 * j
        mix(h0, qk0_ref, bd0_ref)
        gates(h0 + 1, qk1_ref, bd1_ref)
        mix(h0 + 1, qk1_ref, bd1_ref)
        gates(jnp.minimum(h0 + 2, n_heads - 1), qk0_ref, bd0_ref)
        return carry

    lax.fori_loop(0, n_heads // 2, pair_body, 0)


def _hgrn(l, proj, lbs, norm_g, level_mat, batch, seq):
    n_heads = lbs.shape[0]
    t = proj.shape[1]
    tile = HGRN_TILE
    per_seq = seq // tile
    n_chunks = tile // HGRN_CHUNK
    proj5 = proj.reshape(5, n_heads, t, LANES)

    def sec_spec(sec):
        return pl.BlockSpec((None, n_heads, tile, LANES),
                            lambda b, s, l: (sec, 0, b * per_seq + s, 0))

    grid_spec = pltpu.PrefetchScalarGridSpec(
        num_scalar_prefetch=1,
        grid=(batch, per_seq),
        in_specs=[
            sec_spec(0), sec_spec(1), sec_spec(2), sec_spec(3),
            pl.BlockSpec(lbs.shape, lambda b, s, l: (0, 0, 0)),
            pl.BlockSpec((None, 1, HEAD_DIM), lambda b, s, l: (l[0], 0, 0)),
            pl.BlockSpec(level_mat.shape, lambda b, s, l: (0, 0)),
        ],
        out_specs=pl.BlockSpec((n_heads, tile, LANES), lambda b, s, l: (0, b * per_seq + s, 0)),
        scratch_shapes=[pltpu.VMEM((n_heads, HEAD_DIM, HEAD_DIM), F32),
                        pltpu.VMEM((HGRN_CHUNK, HGRN_CHUNK), jnp.int32),
                        pltpu.VMEM((2, tile, HEAD_DIM), BF16),
                        pltpu.VMEM((2, tile, HEAD_DIM), BF16),
                        pltpu.VMEM((n_chunks, level_mat.shape[0], HEAD_DIM), F32),
                        pltpu.VMEM((n_chunks, level_mat.shape[0], HEAD_DIM), F32)],
    )
    return pl.pallas_call(
        _hgrn_kernel,
        grid_spec=grid_spec,
        out_shape=jax.ShapeDtypeStruct((n_heads, t, LANES), BF16),
        compiler_params=_params(2),
        name="hgrn2",
    )(l, proj5, proj5, proj5, proj5, lbs, norm_g, level_mat)


def _pool_group(u, halo, width, pos, pw, ps):
    tile = u.shape[0]
    ext = jnp.concatenate([halo, u], axis=0)
    trow = lax.broadcasted_iota(jnp.int32, (POOL_HALO, 2 * POOL_HALO), 0) + POOL_HALO
    rcol = lax.broadcasted_iota(jnp.int32, (POOL_HALO, 2 * POOL_HALO), 1)
    band = ((rcol <= trow) & (rcol > trow - width)).astype(BF16)
    sums = jnp.concatenate(
        [jnp.dot(band, ext[r * POOL_HALO:(r + 2) * POOL_HALO], preferred_element_type=F32)
         for r in range(tile // POOL_HALO)], axis=0)
    count = jnp.minimum(pos, width).astype(F32)
    pooled = sums / count - u.astype(F32)
    y = jnp.dot(pooled.astype(BF16), pw, preferred_element_type=F32)
    return (y * ps).astype(BF16)


def _outproj_kernel(l_ref, ya_ref, u_ref, pw_ref, ps_ref, w_ref, x_ref, gate_ref, g_ref, sh_ref, sc_ref,
                    o_ref, h_ref, halo_ref, *, tiles_per_seq):
    tile = x_ref.shape[0]
    n_groups = pw_ref.shape[0]
    blocks_per_group = u_ref.shape[0] // n_groups
    tile_in_seq = pl.program_id(0) % tiles_per_seq

    @pl.when(tile_in_seq == 0)
    def _():
        halo_ref[...] = jnp.zeros_like(halo_ref)

    pos = lax.broadcasted_iota(jnp.int32, (tile, 1), 0) + tile_in_seq * tile + 1
    y_parts = [ya_ref[h] for h in range(ya_ref.shape[0])]
    for grp in range(n_groups):
        u = jnp.concatenate([u_ref[grp * blocks_per_group + k] for k in range(blocks_per_group)], axis=1)
        y_parts.append(_pool_group(u, halo_ref[grp], POOL_WINDOWS[grp], pos, pw_ref[grp], ps_ref[grp]))
        halo_ref[grp] = u[tile - POOL_HALO:, :]

    acc = jnp.dot(jnp.concatenate(y_parts, axis=1), w_ref[...], preferred_element_type=F32)
    x1 = x_ref[...] + gate_ref[...] * acc
    o_ref[...] = x1
    h_ref[...] = _norm_mod(x1, g_ref[...], sc_ref[...], sh_ref[...]).astype(BF16)


def _out_proj(l, ya, proj, pool_w, pool_scale, w_out, x2d, gains, mods, seq):
    t, d = x2d.shape
    n_heads = ya.shape[0]
    depth, groups, gd, _ = pool_w.shape
    pool_blocks = groups * gd // LANES
    tm = OUTPROJ_TM
    per_seq = seq // tm

    def mod_spec(k):
        return pl.BlockSpec((None, None, None, 1, d), lambda i, l: (l[0], i // per_seq, k, 0, 0))

    grid_spec = pltpu.PrefetchScalarGridSpec(
        num_scalar_prefetch=1,
        grid=(t // tm,),
        in_specs=[
            pl.BlockSpec((n_heads, tm, LANES), lambda i, l: (0, i, 0)),
            pl.BlockSpec((pool_blocks, tm, LANES), lambda i, l: (proj.shape[0] // pool_blocks - 1, i, 0)),
            pl.BlockSpec((None, groups, gd, gd), lambda i, l: (l[0], 0, 0, 0)),
            pl.BlockSpec((None, groups, 1, gd), lambda i, l: (l[0], 0, 0, 0)),
            pl.BlockSpec((None, d, d), lambda i, l: (l[0], 0, 0)),
            pl.BlockSpec((tm, d), lambda i, l: (i, 0)),
            mod_spec(2),
            pl.BlockSpec((None, 1, d), lambda i, l: (l[0], 0, 0)),
            mod_spec(3), mod_spec(4),
        ],
        out_specs=[pl.BlockSpec((tm, d), lambda i, l: (i, 0)),
                   pl.BlockSpec((tm, d), lambda i, l: (i, 0))],
        scratch_shapes=[pltpu.VMEM((groups, POOL_HALO, gd), BF16)],
    )
    return pl.pallas_call(
        functools.partial(_outproj_kernel, tiles_per_seq=per_seq),
        grid_spec=grid_spec,
        out_shape=[jax.ShapeDtypeStruct((t, d), F32), jax.ShapeDtypeStruct((t, d), BF16)],
        compiler_params=_params(1),
        name="out_proj",
    )(l, ya, proj, pool_w, pool_scale.reshape(depth, groups, 1, gd), w_out, x2d, mods, gains, mods, mods)


def _shift_rows(a, k, halo):
    rolled = pltpu.roll(a, k, axis=0)
    prev = pltpu.roll(halo, k, axis=0)
    ri = lax.broadcasted_iota(jnp.int32, halo.shape, 0)
    top = jnp.where(ri < k, prev, rolled[0:halo.shape[0]])
    return jnp.concatenate([top, rolled[halo.shape[0]:]], axis=0)


def _ffn_kernel(l_ref, x_hbm, h_ref, gate_ref, fg_ref, wa_ref, wv_ref, cw_ref, cb_ref, wd_ref,
                o_hbm, acc_ref, halo_ref, res_ref, sem_ref, *, tiles_per_seq, depth):
    i = pl.program_id(0)
    j = pl.program_id(1)
    n_tiles = pl.num_programs(0)
    n_steps = pl.num_programs(1)
    halo_rows = halo_ref.shape[1]
    tm = acc_ref.shape[0]

    def fetch_rows(tile):
        return pltpu.make_async_copy(x_hbm.at[pl.ds(tile * tm, tm)], res_ref, sem_ref.at[0])

    def write_rows(tile):
        return pltpu.make_async_copy(res_ref, o_hbm.at[pl.ds(tile * tm, tm)], sem_ref.at[1])

    @pl.when(j == n_steps - 2)
    def _():
        @pl.when(i > 0)
        def _():
            write_rows(i - 1).wait()

        fetch_rows(i).start()

    @pl.when(j == 0)
    def _():
        acc_ref[...] = jnp.zeros_like(acc_ref)

    @pl.when(i % tiles_per_seq == 0)
    def _():
        halo_ref[j] = jnp.zeros(halo_ref.shape[1:], F32)

    h = h_ref[...]
    a = jnp.dot(h, wa_ref[...], preferred_element_type=F32)
    v = jnp.dot(h, wv_ref[...], preferred_element_type=F32)
    halo = halo_ref[j]
    cw = cw_ref[...]
    conv = cb_ref[...] + cw[CONV_WIDTH - 1:CONV_WIDTH] * a
    for back in range(1, CONV_WIDTH):
        tap = CONV_WIDTH - 1 - back
        conv = conv + cw[tap:tap + 1] * _shift_rows(a, back, halo)
    halo_ref[j] = a[a.shape[0] - halo_rows:, :]
    glu = (_silu(conv) * v).astype(BF16)
    acc_ref[...] += jnp.dot(glu, wd_ref[...].astype(BF16), preferred_element_type=F32)

    @pl.when(j == n_steps - 1)
    def _():
        fetch_rows(i).wait()
        last_layer = l_ref[0] == depth - 1

        @pl.when(jnp.logical_not(last_layer))
        def _():
            res_ref[...] = res_ref[...] + gate_ref[...] * acc_ref[...]

        @pl.when(last_layer)
        def _():
            xf = res_ref[...] + gate_ref[...] * acc_ref[...]
            ms = jnp.mean(xf * xf, axis=-1, keepdims=True)
            res_ref[...] = xf * lax.rsqrt(ms + EPS) * fg_ref[...]

        write_rows(i).start()

        @pl.when(i == n_tiles - 1)
        def _():
            write_rows(i).wait()


def _ffn(l, x2d, h2d, mods, final_g, w_up, conv_w, conv_b, w_down, seq):
    t, d = x2d.shape
    depth, ff, _ = w_down.shape
    tm, tf = FFN_TM, FFN_TF
    per_seq = seq // tm
    n_ff = ff // tf

    grid_spec = pltpu.PrefetchScalarGridSpec(
        num_scalar_prefetch=1,
        grid=(t // tm, n_ff),
        in_specs=[
            pl.BlockSpec(memory_space=pl.ANY),
            pl.BlockSpec((tm, d), lambda i, j, l: (i, 0)),
            pl.BlockSpec((None, None, None, 1, d), lambda i, j, l: (l[0], i // per_seq, 5, 0, 0)),
            pl.BlockSpec((1, d), lambda i, j, l: (0, 0)),
            pl.BlockSpec((None, d, tf), lambda i, j, l: (l[0], 0, j)),
            pl.BlockSpec((None, d, tf), lambda i, j, l: (l[0], 0, n_ff + j)),
            pl.BlockSpec((None, CONV_WIDTH, tf), lambda i, j, l: (l[0], 0, j)),
            pl.BlockSpec((None, 1, tf), lambda i, j, l: (l[0], 0, j)),
            pl.BlockSpec((None, tf, d), lambda i, j, l: (l[0], j, 0)),
        ],
        out_specs=pl.BlockSpec(memory_space=pl.ANY),
        scratch_shapes=[
            pltpu.VMEM((tm, d), F32),
            pltpu.VMEM((n_ff, 8, tf), F32),
            pltpu.VMEM((tm, d), F32),
            pltpu.SemaphoreType.DMA((2,)),
        ],
    )
    return pl.pallas_call(
        functools.partial(_ffn_kernel, tiles_per_seq=per_seq, depth=depth),
        grid_spec=grid_spec,
        out_shape=jax.ShapeDtypeStruct((t, d), F32),
        compiler_params=_params(2),
        name="conv_glu_ffn",
    )(l, x2d, h2d, mods, final_g, w_up, w_up, conv_w, conv_b.reshape(depth, 1, ff), w_down)


def kernel(x, c, ada_w, ada_b, mix_norm_g, w_in, hgrn_lower_bounds, hgrn_norm_g, pool_w, pool_scale,
           w_out, ffn_norm_g, w_up, conv_w, conv_b, w_down, final_norm_g):
    batch, seq, d = x.shape
    depth = w_in.shape[0]
    d_hgrn = hgrn_lower_bounds.shape[1]
    n_heads = d_hgrn // HEAD_DIM

    mods = _modulation(c, ada_w, ada_b).reshape(depth, batch, N_MOD, 1, d)
    w_in_b, w_out_b = w_in.astype(BF16), w_out.astype(BF16)
    w_up_b, pool_w_b = w_up.astype(BF16), pool_w.astype(BF16)
    lbs = hgrn_lower_bounds.reshape(depth, n_heads, HEAD_DIM).transpose(1, 0, 2)
    mix_g = mix_norm_g.reshape(depth, 1, d)
    ffn_g = ffn_norm_g.reshape(depth, 1, d)
    head_g = hgrn_norm_g.reshape(depth, 1, HEAD_DIM)
    final_g = final_norm_g.reshape(1, d)
    level_mat = jnp.asarray(_hgrn_level_matrix(HGRN_CHUNK), dtype=BF16)

    xc = x.reshape(batch * seq, d)
    for li in range(depth):
        l = jnp.full((1,), li, jnp.int32)
        proj = _in_proj(l, xc, mix_g, mods, w_in_b, seq)
        ya = _hgrn(l, proj, lbs, head_g, level_mat, batch, seq)
        x1, h2 = _out_proj(l, ya, proj, pool_w_b, pool_scale, w_out_b, xc, ffn_g, mods, seq)
        xc = _ffn(l, x1, h2, mods, final_g, w_up_b, conv_w, conv_b, w_down, seq)
    return xc.reshape(batch, seq, d)
```

```python
import functools

import jax
import jax.numpy as jnp
import numpy as np
from jax import lax
from jax.experimental import pallas as pl
from jax.experimental.pallas import tpu as pltpu

F32 = jnp.float32
BF16 = jnp.bfloat16

LANES = 128
HEAD_DIM = 128
POOL_WINDOWS = (2, 4, 8, 16)
POOL_HALO = 128
CONV_WIDTH = 3
N_MOD = 6
EPS = 1e-6
LOG2E = 1.4426950408889634
VMEM_LIMIT = 56 * 1024 * 1024

HGRN_CHUNK = 128
HGRN_TILE = 1024
INPROJ_TM, INPROJ_TN = 512, 1024
OUTPROJ_TM = 512
FFN_TM, FFN_TF = 1024, 512
MOD_TN = 1024


def _params(n_axes, vmem=VMEM_LIMIT):
    return pltpu.CompilerParams(
        dimension_semantics=("arbitrary",) * n_axes, vmem_limit_bytes=vmem)


def _norm_mod(xf, gain, scale, shift):
    ms = jnp.mean(xf * xf, axis=-1, keepdims=True)
    y = xf * lax.rsqrt(ms + EPS) * gain
    return y * (1.0 + scale) + shift


def _silu(x):
    return (0.5 * x) * (1.0 + jnp.tanh(0.5 * x))


def _mod_kernel(c_ref, w_ref, b_ref, o_ref):
    c = c_ref[...]
    c_act = _silu(c).astype(BF16)
    acc = jnp.dot(c_act, w_ref[...].astype(BF16), preferred_element_type=F32)
    o_ref[...] = acc + b_ref[...]


def _modulation(c, ada_w, ada_b):
    depth, d, n = ada_w.shape
    b = c.shape[0]
    return pl.pallas_call(
        _mod_kernel,
        grid=(depth, n // MOD_TN),
        in_specs=[
            pl.BlockSpec((b, d), lambda l, j: (0, 0)),
            pl.BlockSpec((None, d, MOD_TN), lambda l, j: (l, 0, j)),
            pl.BlockSpec((None, 1, MOD_TN), lambda l, j: (l, 0, j)),
        ],
        out_specs=pl.BlockSpec((None, b, MOD_TN), lambda l, j: (l, 0, j)),
        out_shape=jax.ShapeDtypeStruct((depth, b, n), F32),
        compiler_params=_params(2),
        name="ada_mod",
    )(c, ada_w, ada_b.reshape(depth, 1, n))


def _inproj_kernel(l_ref, x_ref, g_ref, sh_ref, sc_ref, w_ref, o_ref):
    h = _norm_mod(x_ref[...], g_ref[...], sc_ref[...], sh_ref[...]).astype(BF16)
    blocks_per_dot = INPROJ_TN // LANES
    for j in range(0, o_ref.shape[0], blocks_per_dot):
        acc = jnp.dot(h, w_ref[:, j * LANES:(j + blocks_per_dot) * LANES], preferred_element_type=F32)
        for blk in range(blocks_per_dot):
            o_ref[j + blk] = acc[:, blk * LANES:(blk + 1) * LANES].astype(BF16)


def _in_proj(l, x2d, gains, mods, w_in, seq):
    t, d = x2d.shape
    n = w_in.shape[2]
    tm = INPROJ_TM
    per_seq = seq // tm
    grid_spec = pltpu.PrefetchScalarGridSpec(
        num_scalar_prefetch=1,
        grid=(t // tm,),
        in_specs=[
            pl.BlockSpec((tm, d), lambda i, l: (i, 0)),
            pl.BlockSpec((None, 1, d), lambda i, l: (l[0], 0, 0)),
            pl.BlockSpec((None, None, None, 1, d), lambda i, l: (l[0], i // per_seq, 0, 0, 0)),
            pl.BlockSpec((None, None, None, 1, d), lambda i, l: (l[0], i // per_seq, 1, 0, 0)),
            pl.BlockSpec((None, d, n), lambda i, l: (l[0], 0, 0), pipeline_mode=pl.Buffered(1)),
        ],
        out_specs=pl.BlockSpec((n // LANES, tm, LANES), lambda i, l: (0, i, 0)),
    )
    return pl.pallas_call(
        _inproj_kernel,
        grid_spec=grid_spec,
        out_shape=jax.ShapeDtypeStruct((n // LANES, t, LANES), BF16),
        compiler_params=_params(1),
        name="in_proj",
    )(l, x2d, gains, mods, mods, w_in)


def _hgrn_level_matrix(chunk):
    t = np.arange(chunk)[:, None]
    r = np.arange(chunk)[None, :]
    low = (r <= t).astype(np.float32)
    mats = [low]
    m = 1
    while m < chunk:
        anchor = (t // (2 * m)) * (2 * m) + m - 1
        sign = np.where(t > anchor, 1.0, -1.0)
        mats.append(sign * (low - (r <= anchor).astype(np.float32)))
        m *= 2
    stacked = np.concatenate(mats, axis=0)
    return np.concatenate([stacked] * 2, axis=1)


_NT = (((1,), (1,)), ((), ()))
_TN = (((0,), (0,)), ((), ()))


def _hgrn_kernel(l_ref, q_ref, f_ref, i_ref, g_ref, lb_ref, ng_ref, lm_ref, o_ref,
                 st_ref, lvl_ref, qk0_ref, qk1_ref, bd0_ref, bd1_ref):
    chunk = HGRN_CHUNK
    n_levels = chunk.bit_length() - 1
    n_heads, tile, _ = q_ref.shape
    n_chunks = tile // chunk
    layer = l_ref[0]

    @pl.when((pl.program_id(0) == 0) & (pl.program_id(1) == 0))
    def _():
        trow = lax.broadcasted_iota(jnp.int32, (chunk, chunk), 0)
        tcol = lax.broadcasted_iota(jnp.int32, (chunk, chunk), 1)
        txor = trow ^ tcol
        top = jnp.full((chunk, chunk), -1, jnp.int32)
        for k in range(n_levels):
            top = top + ((txor >> k) != 0).astype(jnp.int32)
        lvl_ref[...] = jnp.where(trow > tcol, top, jnp.where(trow == tcol, n_levels, -1))

    @pl.when(pl.program_id(1) == 0)
    def _():
        st_ref[...] = jnp.zeros_like(st_ref)

    gain = ng_ref[...]

    def gates(h, qk_ref, bd_ref):
        lbraw = lb_ref[h]
        ex = jnp.exp(lbraw - jnp.max(lbraw, axis=0, keepdims=True))
        p = ex / jnp.sum(ex, axis=0, keepdims=True)
        lrow = lax.broadcasted_iota(jnp.int32, p.shape, 0)
        cum = jnp.sum(jnp.where(lrow <= layer, p, 0.0), axis=0, keepdims=True)
        lb = jnp.clip(cum - p[0:1], 0.0, 1.0)
        half_key = 0.5 * (1.0 - lb)

        parts = []
        for c in range(n_chunks):
            rows = pl.ds(c * chunk, chunk)
            z = f_ref[h, rows, :].astype(F32)
            e = jnp.exp2(z * (-LOG2E))
            logf = (jnp.log(1.0 + lb * e) - jnp.log(1.0 + e)) * LOG2E
            key = half_key - half_key * jnp.tanh(0.5 * z)
            qraw = q_ref[h, rows, :].astype(F32)
            qq = (qraw * (0.5 * HEAD_DIM ** -0.5)) * (1.0 + jnp.tanh(0.5 * qraw))
            qk_ref[0, rows, :] = qq.astype(BF16)
            qk_ref[1, rows, :] = key.astype(BF16)
            hi = logf.astype(BF16)
            lo = (logf - hi.astype(F32)).astype(BF16)
            parts.append(jnp.concatenate([hi, lo], axis=0))

        bd = jnp.dot(lm_ref[...], jnp.concatenate(parts, axis=1), preferred_element_type=F32)
        for c in range(n_chunks):
            bd_ref[c] = bd[:, c * HEAD_DIM:(c + 1) * HEAD_DIM]

    def mix(h, qk_ref, bd_ref):
        st = st_ref[h]
        lvl = lvl_ref[...]
        all_scores = []
        for c in range(n_chunks):
            rows = pl.ds(c * chunk, chunk)
            s0 = lax.dot_general(qk_ref[0, rows, :], qk_ref[1, rows, :], _NT, preferred_element_type=F32)
            all_scores.append(jnp.where(lvl == n_levels, s0, 0.0))
        for k in range(n_levels):
            for c in range(n_chunks):
                rows = pl.ds(c * chunk, chunk)
                w = jnp.exp2(bd_ref[c, pl.ds(chunk * (k + 1), chunk), :]).astype(BF16)
                sk = lax.dot_general(qk_ref[0, rows, :] * w, qk_ref[1, rows, :] * w, _NT,
                                     preferred_element_type=F32)
                all_scores[c] = jnp.where(lvl == k, sk, all_scores[c])
        all_scores = [s.astype(BF16) for s in all_scores]

        for c in range(n_chunks):
            rows = pl.ds(c * chunk, chunk)
            qq = qk_ref[0, rows, :]
            key = qk_ref[1, rows, :]
            val = i_ref[h, rows, :]
            b = bd_ref[c, pl.ds(0, chunk), :]
            o = jnp.dot(all_scores[c], val, preferred_element_type=F32)
            o = o + lax.dot_general(qq * jnp.exp2(b).astype(BF16), st.astype(BF16), _NT,
                                    preferred_element_type=F32)

            b_last = b[chunk - 1:chunk]
            k_out = key * jnp.exp2(b_last - b).astype(BF16)
            st = st * jnp.exp2(b_last) + lax.dot_general(val, k_out, _TN, preferred_element_type=F32)

            ms = jnp.mean(o * o, axis=-1, keepdims=True)
            on = o * lax.rsqrt(ms + EPS) * gain
            graw = g_ref[h, rows, :].astype(F32)
            o_ref[h, rows, :] = (on * _silu(graw)).astype(BF16)
        st_ref[h] = st

    gates(0, qk0_ref, bd0_ref)

    def pair_body(j, carry):
        h0 = 2 * j
        mix(h0, qk0_ref, bd0_ref)
        gates(h0 + 1, qk1_ref, bd1_ref)
        mix(h0 + 1, qk1_ref, bd1_ref)
        gates(h0 + 2, qk0_ref, bd0_ref)
        return carry

    lax.fori_loop(0, n_heads // 2 - 1, pair_body, 0)
    mix(n_heads - 2, qk0_ref, bd0_ref)
    gates(n_heads - 1, qk1_ref, bd1_ref)
    mix(n_heads - 1, qk1_ref, bd1_ref)


def _hgrn(l, proj, lbs, norm_g, level_mat, batch, seq):
    n_heads = lbs.shape[0]
    t = proj.shape[1]
    tile = HGRN_TILE
    per_seq = seq // tile
    n_chunks = tile // HGRN_CHUNK
    proj5 = proj.reshape(5, n_heads, t, LANES)

    def sec_spec(sec):
        return pl.BlockSpec((None, n_heads, tile, LANES),
                            lambda b, s, l: (sec, 0, b * per_seq + s, 0))

    grid_spec = pltpu.PrefetchScalarGridSpec(
        num_scalar_prefetch=1,
        grid=(batch, per_seq),
        in_specs=[
            sec_spec(0), sec_spec(1), sec_spec(2), sec_spec(3),
            pl.BlockSpec(lbs.shape, lambda b, s, l: (0, 0, 0)),
            pl.BlockSpec((None, 1, HEAD_DIM), lambda b, s, l: (l[0], 0, 0)),
            pl.BlockSpec(level_mat.shape, lambda b, s, l: (0, 0)),
        ],
        out_specs=pl.BlockSpec((n_heads, tile, LANES), lambda b, s, l: (0, b * per_seq + s, 0)),
        scratch_shapes=[pltpu.VMEM((n_heads, HEAD_DIM, HEAD_DIM), F32),
                        pltpu.VMEM((HGRN_CHUNK, HGRN_CHUNK), jnp.int32),
                        pltpu.VMEM((2, tile, HEAD_DIM), BF16),
                        pltpu.VMEM((2, tile, HEAD_DIM), BF16),
                        pltpu.VMEM((n_chunks, level_mat.shape[0], HEAD_DIM), F32),
                        pltpu.VMEM((n_chunks, level_mat.shape[0], HEAD_DIM), F32)],
    )
    return pl.pallas_call(
        _hgrn_kernel,
        grid_spec=grid_spec,
        out_shape=jax.ShapeDtypeStruct((n_heads, t, LANES), BF16),
        compiler_params=_params(2),
        name="hgrn2",
    )(l, proj5, proj5, proj5, proj5, lbs, norm_g, level_mat)


def _pool_group(u, halo, width, pos, pw, ps):
    tile = u.shape[0]
    ext = jnp.concatenate([halo, u], axis=0)
    trow = lax.broadcasted_iota(jnp.int32, (POOL_HALO, 2 * POOL_HALO), 0) + POOL_HALO
    rcol = lax.broadcasted_iota(jnp.int32, (POOL_HALO, 2 * POOL_HALO), 1)
    band = ((rcol <= trow) & (rcol > trow - width)).astype(BF16)
    sums = jnp.concatenate(
        [jnp.dot(band, ext[r * POOL_HALO:(r + 2) * POOL_HALO], preferred_element_type=F32)
         for r in range(tile // POOL_HALO)], axis=0)
    count = jnp.minimum(pos, width).astype(F32)
    pooled = sums / count - u.astype(F32)
    y = jnp.dot(pooled.astype(BF16), pw, preferred_element_type=F32)
    return (y * ps).astype(BF16)


def _outproj_kernel(l_ref, ya_ref, u_ref, pw_ref, ps_ref, w_ref, x_ref, gate_ref, g_ref, sh_ref, sc_ref,
                    o_ref, h_ref, halo_ref, *, tiles_per_seq):
    tile = x_ref.shape[0]
    n_groups = pw_ref.shape[0]
    blocks_per_group = u_ref.shape[0] // n_groups
    tile_in_seq = pl.program_id(0) % tiles_per_seq

    @pl.when(tile_in_seq == 0)
    def _():
        halo_ref[...] = jnp.zeros_like(halo_ref)

    pos = lax.broadcasted_iota(jnp.int32, (tile, 1), 0) + tile_in_seq * tile + 1
    y_parts = [ya_ref[h] for h in range(ya_ref.shape[0])]
    for grp in range(n_groups):
        u = jnp.concatenate([u_ref[grp * blocks_per_group + k] for k in range(blocks_per_group)], axis=1)
        y_parts.append(_pool_group(u, halo_ref[grp], POOL_WINDOWS[grp], pos, pw_ref[grp], ps_ref[grp]))
        halo_ref[grp] = u[tile - POOL_HALO:, :]

    acc = jnp.dot(jnp.concatenate(y_parts, axis=1), w_ref[...], preferred_element_type=F32)
    x1 = x_ref[...] + gate_ref[...] * acc
    o_ref[...] = x1
    h_ref[...] = _norm_mod(x1, g_ref[...], sc_ref[...], sh_ref[...]).astype(BF16)


def _out_proj(l, ya, proj, pool_w, pool_scale, w_out, x2d, gains, mods, seq):
    t, d = x2d.shape
    n_heads = ya.shape[0]
    depth, groups, gd, _ = pool_w.shape
    pool_blocks = groups * gd // LANES
    tm = OUTPROJ_TM
    per_seq = seq // tm

    def mod_spec(k):
        return pl.BlockSpec((None, None, None, 1, d), lambda i, l: (l[0], i // per_seq, k, 0, 0))

    grid_spec = pltpu.PrefetchScalarGridSpec(
        num_scalar_prefetch=1,
        grid=(t // tm,),
        in_specs=[
            pl.BlockSpec((n_heads, tm, LANES), lambda i, l: (0, i, 0)),
            pl.BlockSpec((pool_blocks, tm, LANES), lambda i, l: (proj.shape[0] // pool_blocks - 1, i, 0)),
            pl.BlockSpec((None, groups, gd, gd), lambda i, l: (l[0], 0, 0, 0)),
            pl.BlockSpec((None, groups, 1, gd), lambda i, l: (l[0], 0, 0, 0)),
            pl.BlockSpec((None, d, d), lambda i, l: (l[0], 0, 0)),
            pl.BlockSpec((tm, d), lambda i, l: (i, 0)),
            mod_spec(2),
            pl.BlockSpec((None, 1, d), lambda i, l: (l[0], 0, 0)),
            mod_spec(3), mod_spec(4),
        ],
        out_specs=[pl.BlockSpec((tm, d), lambda i, l: (i, 0)),
                   pl.BlockSpec((tm, d), lambda i, l: (i, 0))],
        scratch_shapes=[pltpu.VMEM((groups, POOL_HALO, gd), BF16)],
    )
    return pl.pallas_call(
        functools.partial(_outproj_kernel, tiles_per_seq=per_seq),
        grid_spec=grid_spec,
        out_shape=[jax.ShapeDtypeStruct((t, d), F32), jax.ShapeDtypeStruct((t, d), BF16)],
        compiler_params=_params(1),
        name="out_proj",
    )(l, ya, proj, pool_w, pool_scale.reshape(depth, groups, 1, gd), w_out, x2d, mods, gains, mods, mods)


def _shift_rows(a, k, halo):
    rolled = pltpu.roll(a, k, axis=0)
    prev = pltpu.roll(halo, k, axis=0)
    ri = lax.broadcasted_iota(jnp.int32, halo.shape, 0)
    top = jnp.where(ri < k, prev, rolled[0:halo.shape[0]])
    return jnp.concatenate([top, rolled[halo.shape[0]:]], axis=0)


def _ffn_kernel(l_ref, x_hbm, h_ref, gate_ref, fg_ref, wa_ref, wv_ref, cw_ref, cb_ref, wd_ref,
                o_hbm, acc_ref, halo_ref, res_ref, sem_ref, *, tiles_per_seq, depth):
    i = pl.program_id(0)
    j = pl.program_id(1)
    n_tiles = pl.num_programs(0)
    n_steps = pl.num_programs(1)
    halo_rows = halo_ref.shape[1]
    tm = acc_ref.shape[0]

    def fetch_rows(tile):
        return pltpu.make_async_copy(x_hbm.at[pl.ds(tile * tm, tm)], res_ref, sem_ref.at[0])

    def write_rows(tile):
        return pltpu.make_async_copy(res_ref, o_hbm.at[pl.ds(tile * tm, tm)], sem_ref.at[1])

    @pl.when(j == n_steps - 2)
    def _():
        @pl.when(i > 0)
        def _():
            write_rows(i - 1).wait()

        fetch_rows(i).start()

    @pl.when(j == 0)
    def _():
        acc_ref[...] = jnp.zeros_like(acc_ref)

    @pl.when(i % tiles_per_seq == 0)
    def _():
        halo_ref[j] = jnp.zeros(halo_ref.shape[1:], F32)

    h = h_ref[...]
    a = jnp.dot(h, wa_ref[...], preferred_element_type=F32)
    v = jnp.dot(h, wv_ref[...], preferred_element_type=F32)
    halo = halo_ref[j]
    cw = cw_ref[...]
    conv = cb_ref[...] + cw[CONV_WIDTH - 1:CONV_WIDTH] * a
    for back in range(1, CONV_WIDTH):
        tap = CONV_WIDTH - 1 - back
        conv = conv + cw[tap:tap + 1] * _shift_rows(a, back, halo)
    halo_ref[j] = a[a.shape[0] - halo_rows:, :]
    glu = (_silu(conv) * v).astype(BF16)
    acc_ref[...] += jnp.dot(glu, wd_ref[...].astype(BF16), preferred_element_type=F32)

    @pl.when(j == n_steps - 1)
    def _():
        fetch_rows(i).wait()
        last_layer = l_ref[0] == depth - 1

        @pl.when(jnp.logical_not(last_layer))
        def _():
            res_ref[...] = res_ref[...] + gate_ref[...] * acc_ref[...]

        @pl.when(last_layer)
        def _():
            xf = res_ref[...] + gate_ref[...] * acc_ref[...]
            ms = jnp.mean(xf * xf, axis=-1, keepdims=True)
            res_ref[...] = xf * lax.rsqrt(ms + EPS) * fg_ref[...]

        write_rows(i).start()

        @pl.when(i == n_tiles - 1)
        def _():
            write_rows(i).wait()


def _ffn(l, x2d, h2d, mods, final_g, w_up, conv_w, conv_b, w_down, seq):
    t, d = x2d.shape
    depth, ff, _ = w_down.shape
    tm, tf = FFN_TM, FFN_TF
    per_seq = seq // tm
    n_ff = ff // tf

    grid_spec = pltpu.PrefetchScalarGridSpec(
        num_scalar_prefetch=1,
        grid=(t // tm, n_ff),
        in_specs=[
            pl.BlockSpec(memory_space=pl.ANY),
            pl.BlockSpec((tm, d), lambda i, j, l: (i, 0)),
            pl.BlockSpec((None, None, None, 1, d), lambda i, j, l: (l[0], i // per_seq, 5, 0, 0)),
            pl.BlockSpec((1, d), lambda i, j, l: (0, 0)),
            pl.BlockSpec((None, d, tf), lambda i, j, l: (l[0], 0, j)),
            pl.BlockSpec((None, d, tf), lambda i, j, l: (l[0], 0, n_ff + j)),
            pl.BlockSpec((None, CONV_WIDTH, tf), lambda i, j, l: (l[0], 0, j)),
            pl.BlockSpec((None, 1, tf), lambda i, j, l: (l[0], 0, j)),
            pl.BlockSpec((None, tf, d), lambda i, j, l: (l[0], j, 0)),
        ],
        out_specs=pl.BlockSpec(memory_space=pl.ANY),
        scratch_shapes=[
            pltpu.VMEM((tm, d), F32),
            pltpu.VMEM((n_ff, 8, tf), F32),
            pltpu.VMEM((tm, d), F32),
            pltpu.SemaphoreType.DMA((2,)),
        ],
    )
    return pl.pallas_call(
        functools.partial(_ffn_kernel, tiles_per_seq=per_seq, depth=depth),
        grid_spec=grid_spec,
        out_shape=jax.ShapeDtypeStruct((t, d), F32),
        compiler_params=_params(2),
        name="conv_glu_ffn",
    )(l, x2d, h2d, mods, final_g, w_up, w_up, conv_w, conv_b.reshape(depth, 1, ff), w_down)


def kernel(x, c, ada_w, ada_b, mix_norm_g, w_in, hgrn_lower_bounds, hgrn_norm_g, pool_w, pool_scale,
           w_out, ffn_norm_g, w_up, conv_w, conv_b, w_down, final_norm_g):
    batch, seq, d = x.shape
    depth = w_in.shape[0]
    d_hgrn = hgrn_lower_bounds.shape[1]
    n_heads = d_hgrn // HEAD_DIM

    mods = _modulation(c, ada_w, ada_b).reshape(depth, batch, N_MOD, 1, d)
    w_in_b, w_out_b = w_in.astype(BF16), w_out.astype(BF16)
    w_up_b, pool_w_b = w_up.astype(BF16), pool_w.astype(BF16)
    lbs = hgrn_lower_bounds.reshape(depth, n_heads, HEAD_DIM).transpose(1, 0, 2)
    mix_g = mix_norm_g.reshape(depth, 1, d)
    ffn_g = ffn_norm_g.reshape(depth, 1, d)
    head_g = hgrn_norm_g.reshape(depth, 1, HEAD_DIM)
    final_g = final_norm_g.reshape(1, d)
    level_mat = jnp.asarray(_hgrn_level_matrix(HGRN_CHUNK), dtype=BF16)

    xc = x.reshape(batch * seq, d)
    for li in range(depth):
        l = jnp.full((1,), li, jnp.int32)
        proj = _in_proj(l, xc, mix_g, mods, w_in_b, seq)
        ya = _hgrn(l, proj, lbs, head_g, level_mat, batch, seq)
        x1, h2 = _out_proj(l, ya, proj, pool_w_b, pool_scale, w_out_b, xc, ffn_g, mods, seq)
        xc = _ffn(l, x1, h2, mods, final_g, w_up_b, conv_w, conv_b, w_down, seq)
    return xc.reshape(batch, seq, d)
```

```python
import functools

import jax
import jax.numpy as jnp
import numpy as np
from jax import lax
from jax.experimental import pallas as pl
from jax.experimental.pallas import tpu as pltpu

F32 = jnp.float32
BF16 = jnp.bfloat16

LANES = 128
HEAD_DIM = 128
POOL_WINDOWS = (2, 4, 8, 16)
POOL_HALO = 128
CONV_WIDTH = 3
N_MOD = 6
EPS = 1e-6
LOG2E = 1.4426950408889634
VMEM_LIMIT = 56 * 1024 * 1024

HGRN_CHUNK = 128
HGRN_TILE = 1024
INPROJ_TM, INPROJ_TN = 512, 1024
OUTPROJ_TM = 512
FFN_TM, FFN_TF = 1024, 512
MOD_TN = 2048


def _params(n_axes, vmem=VMEM_LIMIT):
    return pltpu.CompilerParams(
        dimension_semantics=("arbitrary",) * n_axes, vmem_limit_bytes=vmem)


def _norm_mod(xf, gain, scale, shift):
    ms = jnp.mean(xf * xf, axis=-1, keepdims=True)
    y = xf * lax.rsqrt(ms + EPS) * gain
    return y * (1.0 + scale) + shift


def _silu(x):
    return (0.5 * x) * (1.0 + jnp.tanh(0.5 * x))


def _mod_kernel(c_ref, w_ref, b_ref, o_ref):
    c = c_ref[...]
    c_act = _silu(c).astype(BF16)
    acc = jnp.dot(c_act, w_ref[...].astype(BF16), preferred_element_type=F32)
    o_ref[...] = acc + b_ref[...]


def _modulation(c, ada_w, ada_b):
    depth, d, n = ada_w.shape
    b = c.shape[0]
    return pl.pallas_call(
        _mod_kernel,
        grid=(depth, n // MOD_TN),
        in_specs=[
            pl.BlockSpec((b, d), lambda l, j: (0, 0)),
            pl.BlockSpec((None, d, MOD_TN), lambda l, j: (l, 0, j)),
            pl.BlockSpec((None, 1, MOD_TN), lambda l, j: (l, 0, j)),
        ],
        out_specs=pl.BlockSpec((None, b, MOD_TN), lambda l, j: (l, 0, j)),
        out_shape=jax.ShapeDtypeStruct((depth, b, n), F32),
        compiler_params=_params(2),
        name="ada_mod",
    )(c, ada_w, ada_b.reshape(depth, 1, n))


def _inproj_kernel(l_ref, x_ref, g_ref, sh_ref, sc_ref, w_ref, o_ref):
    h = _norm_mod(x_ref[...], g_ref[...], sc_ref[...], sh_ref[...]).astype(BF16)
    blocks_per_dot = INPROJ_TN // LANES
    for j in range(0, o_ref.shape[0], blocks_per_dot):
        acc = jnp.dot(h, w_ref[:, j * LANES:(j + blocks_per_dot) * LANES], preferred_element_type=F32)
        for blk in range(blocks_per_dot):
            o_ref[j + blk] = acc[:, blk * LANES:(blk + 1) * LANES].astype(BF16)


def _in_proj(l, x2d, gains, mods, w_in, seq):
    t, d = x2d.shape
    n = w_in.shape[2]
    tm = INPROJ_TM
    per_seq = seq // tm
    grid_spec = pltpu.PrefetchScalarGridSpec(
        num_scalar_prefetch=1,
        grid=(t // tm,),
        in_specs=[
            pl.BlockSpec((tm, d), lambda i, l: (i, 0)),
            pl.BlockSpec((None, 1, d), lambda i, l: (l[0], 0, 0)),
            pl.BlockSpec((None, None, None, 1, d), lambda i, l: (l[0], i // per_seq, 0, 0, 0)),
            pl.BlockSpec((None, None, None, 1, d), lambda i, l: (l[0], i // per_seq, 1, 0, 0)),
            pl.BlockSpec((None, d, n), lambda i, l: (l[0], 0, 0), pipeline_mode=pl.Buffered(1)),
        ],
        out_specs=pl.BlockSpec((n // LANES, tm, LANES), lambda i, l: (0, i, 0)),
    )
    return pl.pallas_call(
        _inproj_kernel,
        grid_spec=grid_spec,
        out_shape=jax.ShapeDtypeStruct((n // LANES, t, LANES), BF16),
        compiler_params=_params(1),
        name="in_proj",
    )(l, x2d, gains, mods, mods, w_in)


def _hgrn_level_matrix(chunk):
    t = np.arange(chunk)[:, None]
    r = np.arange(chunk)[None, :]
    low = (r <= t).astype(np.float32)
    mats = [low]
    m = 1
    while m < chunk:
        anchor = (t // (2 * m)) * (2 * m) + m - 1
        sign = np.where(t > anchor, 1.0, -1.0)
        mats.append(sign * (low - (r <= anchor).astype(np.float32)))
        m *= 2
    stacked = np.concatenate(mats, axis=0)
    return np.concatenate([stacked] * 2, axis=1)


_NT = (((1,), (1,)), ((), ()))
_TN = (((0,), (0,)), ((), ()))


def _hgrn_kernel(l_ref, q_ref, f_ref, i_ref, g_ref, lb_ref, ng_ref, lm_ref, o_ref,
                 st_ref, lvl_ref, qk0_ref, qk1_ref, bd0_ref, bd1_ref):
    chunk = HGRN_CHUNK
    n_levels = chunk.bit_length() - 1
    n_heads, tile, _ = q_ref.shape
    n_chunks = tile // chunk
    layer = l_ref[0]

    @pl.when((pl.program_id(0) == 0) & (pl.program_id(1) == 0))
    def _():
        trow = lax.broadcasted_iota(jnp.int32, (chunk, chunk), 0)
        tcol = lax.broadcasted_iota(jnp.int32, (chunk, chunk), 1)
        txor = trow ^ tcol
        top = jnp.full((chunk, chunk), -1, jnp.int32)
        for k in range(n_levels):
            top = top + ((txor >> k) != 0).astype(jnp.int32)
        lvl_ref[...] = jnp.where(trow > tcol, top, jnp.where(trow == tcol, n_levels, -1))

    @pl.when(pl.program_id(1) == 0)
    def _():
        st_ref[...] = jnp.zeros_like(st_ref)

    gain = ng_ref[...]

    def gates(h, qk_ref, bd_ref):
        lbraw = lb_ref[h]
        ex = jnp.exp(lbraw - jnp.max(lbraw, axis=0, keepdims=True))
        p = ex / jnp.sum(ex, axis=0, keepdims=True)
        lrow = lax.broadcasted_iota(jnp.int32, p.shape, 0)
        cum = jnp.sum(jnp.where(lrow <= layer, p, 0.0), axis=0, keepdims=True)
        lb = jnp.clip(cum - p[0:1], 0.0, 1.0)
        half_key = 0.5 * (1.0 - lb)

        parts = []
        for c in range(n_chunks):
            rows = pl.ds(c * chunk, chunk)
            z = f_ref[h, rows, :].astype(F32)
            e = jnp.exp2(z * (-LOG2E))
            logf = (jnp.log(1.0 + lb * e) - jnp.log(1.0 + e)) * LOG2E
            key = half_key - half_key * jnp.tanh(0.5 * z)
            qraw = q_ref[h, rows, :].astype(F32)
            qq = (qraw * (0.5 * HEAD_DIM ** -0.5)) * (1.0 + jnp.tanh(0.5 * qraw))
            qk_ref[0, rows, :] = qq.astype(BF16)
            qk_ref[1, rows, :] = key.astype(BF16)
            hi = logf.astype(BF16)
            lo = (logf - hi.astype(F32)).astype(BF16)
            parts.append(jnp.concatenate([hi, lo], axis=0))

        bd = jnp.dot(lm_ref[...], jnp.concatenate(parts, axis=1), preferred_element_type=F32)
        for c in range(n_chunks):
            bd_ref[c] = bd[:, c * HEAD_DIM:(c + 1) * HEAD_DIM]

    def mix(h, qk_ref, bd_ref):
        st = st_ref[h]
        lvl = lvl_ref[...]
        all_scores = []
        for c in range(n_chunks):
            rows = pl.ds(c * chunk, chunk)
            qq = qk_ref[0, rows, :]
            key = qk_ref[1, rows, :]
            s0 = lax.dot_general(qq, key, _NT, preferred_element_type=F32)
            scores = jnp.where(lvl == n_levels, s0, 0.0)
            for k in range(n_levels):
                w = jnp.exp2(bd_ref[c, pl.ds(chunk * (k + 1), chunk), :]).astype(BF16)
                sk = lax.dot_general(qq * w, key * w, _NT, preferred_element_type=F32)
                scores = jnp.where(lvl == k, sk, scores)
            all_scores.append(scores.astype(BF16))

        for c in range(n_chunks):
            rows = pl.ds(c * chunk, chunk)
            qq = qk_ref[0, rows, :]
            key = qk_ref[1, rows, :]
            val = i_ref[h, rows, :]
            b = bd_ref[c, pl.ds(0, chunk), :]
            o = jnp.dot(all_scores[c], val, preferred_element_type=F32)
            o = o + lax.dot_general(qq * jnp.exp2(b).astype(BF16), st.astype(BF16), _NT,
                                    preferred_element_type=F32)

            b_last = b[chunk - 1:chunk]
            k_out = key * jnp.exp2(b_last - b).astype(BF16)
            st = st * jnp.exp2(b_last) + lax.dot_general(val, k_out, _TN, preferred_element_type=F32)

            ms = jnp.mean(o * o, axis=-1, keepdims=True)
            on = o * lax.rsqrt(ms + EPS) * gain
            graw = g_ref[h, rows, :].astype(F32)
            o_ref[h, rows, :] = (on * _silu(graw)).astype(BF16)
        st_ref[h] = st

    gates(0, qk0_ref, bd0_ref)

    def pair_body(j, carry):
        h0 = 2 * j
        mix(h0, qk0_ref, bd0_ref)
        gates(h0 + 1, qk1_ref, bd1_ref)
        mix(h0 + 1, qk1_ref, bd1_ref)
        gates(h0 + 2, qk0_ref, bd0_ref)
        return carry

    lax.fori_loop(0, n_heads // 2 - 1, pair_body, 0)
    mix(n_heads - 2, qk0_ref, bd0_ref)
    gates(n_heads - 1, qk1_ref, bd1_ref)
    mix(n_heads - 1, qk1_ref, bd1_ref)


def _hgrn(l, proj, lbs, norm_g, level_mat, batch, seq):
    n_heads = lbs.shape[0]
    t = proj.shape[1]
    tile = HGRN_TILE
    per_seq = seq // tile
    n_chunks = tile // HGRN_CHUNK
    proj5 = proj.reshape(5, n_heads, t, LANES)

    def sec_spec(sec):
        return pl.BlockSpec((None, n_heads, tile, LANES),
                            lambda b, s, l: (sec, 0, b * per_seq + s, 0))

    grid_spec = pltpu.PrefetchScalarGridSpec(
        num_scalar_prefetch=1,
        grid=(batch, per_seq),
        in_specs=[
            sec_spec(0), sec_spec(1), sec_spec(2), sec_spec(3),
            pl.BlockSpec(lbs.shape, lambda b, s, l: (0, 0, 0)),
            pl.BlockSpec((None, 1, HEAD_DIM), lambda b, s, l: (l[0], 0, 0)),
            pl.BlockSpec(level_mat.shape, lambda b, s, l: (0, 0)),
        ],
        out_specs=pl.BlockSpec((n_heads, tile, LANES), lambda b, s, l: (0, b * per_seq + s, 0)),
        scratch_shapes=[pltpu.VMEM((n_heads, HEAD_DIM, HEAD_DIM), F32),
                        pltpu.VMEM((HGRN_CHUNK, HGRN_CHUNK), jnp.int32),
                        pltpu.VMEM((2, tile, HEAD_DIM), BF16),
                        pltpu.VMEM((2, tile, HEAD_DIM), BF16),
                        pltpu.VMEM((n_chunks, level_mat.shape[0], HEAD_DIM), F32),
                        pltpu.VMEM((n_chunks, level_mat.shape[0], HEAD_DIM), F32)],
    )
    return pl.pallas_call(
        _hgrn_kernel,
        grid_spec=grid_spec,
        out_shape=jax.ShapeDtypeStruct((n_heads, t, LANES), BF16),
        compiler_params=_params(2),
        name="hgrn2",
    )(l, proj5, proj5, proj5, proj5, lbs, norm_g, level_mat)


def _pool_group(u, halo, width, pos, pw, ps):
    tile = u.shape[0]
    ext = jnp.concatenate([halo, u], axis=0)
    trow = lax.broadcasted_iota(jnp.int32, (POOL_HALO, 2 * POOL_HALO), 0) + POOL_HALO
    rcol = lax.broadcasted_iota(jnp.int32, (POOL_HALO, 2 * POOL_HALO), 1)
    band = ((rcol <= trow) & (rcol > trow - width)).astype(BF16)
    sums = jnp.concatenate(
        [jnp.dot(band, ext[r * POOL_HALO:(r + 2) * POOL_HALO], preferred_element_type=F32)
         for r in range(tile // POOL_HALO)], axis=0)
    count = jnp.minimum(pos, width).astype(F32)
    pooled = sums / count - u.astype(F32)
    y = jnp.dot(pooled.astype(BF16), pw, preferred_element_type=F32)
    return (y * ps).astype(BF16)


def _outproj_kernel(l_ref, ya_ref, u_ref, pw_ref, ps_ref, w_ref, x_ref, gate_ref, g_ref, sh_ref, sc_ref,
                    o_ref, h_ref, halo_ref, *, tiles_per_seq):
    tile = x_ref.shape[0]
    n_groups = pw_ref.shape[0]
    blocks_per_group = u_ref.shape[0] // n_groups
    tile_in_seq = pl.program_id(0) % tiles_per_seq

    @pl.when(tile_in_seq == 0)
    def _():
        halo_ref[...] = jnp.zeros_like(halo_ref)

    pos = lax.broadcasted_iota(jnp.int32, (tile, 1), 0) + tile_in_seq * tile + 1
    y_parts = [ya_ref[h] for h in range(ya_ref.shape[0])]
    for grp in range(n_groups):
        u = jnp.concatenate([u_ref[grp * blocks_per_group + k] for k in range(blocks_per_group)], axis=1)
        y_parts.append(_pool_group(u, halo_ref[grp], POOL_WINDOWS[grp], pos, pw_ref[grp], ps_ref[grp]))
        halo_ref[grp] = u[tile - POOL_HALO:, :]

    acc = jnp.dot(jnp.concatenate(y_parts, axis=1), w_ref[...], preferred_element_type=F32)
    x1 = x_ref[...] + gate_ref[...] * acc
    o_ref[...] = x1
    h_ref[...] = _norm_mod(x1, g_ref[...], sc_ref[...], sh_ref[...]).astype(BF16)


def _out_proj(l, ya, proj, pool_w, pool_scale, w_out, x2d, gains, mods, seq):
    t, d = x2d.shape
    n_heads = ya.shape[0]
    depth, groups, gd, _ = pool_w.shape
    pool_blocks = groups * gd // LANES
    tm = OUTPROJ_TM
    per_seq = seq // tm

    def mod_spec(k):
        return pl.BlockSpec((None, None, None, 1, d), lambda i, l: (l[0], i // per_seq, k, 0, 0))

    grid_spec = pltpu.PrefetchScalarGridSpec(
        num_scalar_prefetch=1,
        grid=(t // tm,),
        in_specs=[
            pl.BlockSpec((n_heads, tm, LANES), lambda i, l: (0, i, 0)),
            pl.BlockSpec((pool_blocks, tm, LANES), lambda i, l: (proj.shape[0] // pool_blocks - 1, i, 0)),
            pl.BlockSpec((None, groups, gd, gd), lambda i, l: (l[0], 0, 0, 0)),
            pl.BlockSpec((None, groups, 1, gd), lambda i, l: (l[0], 0, 0, 0)),
            pl.BlockSpec((None, d, d), lambda i, l: (l[0], 0, 0)),
            pl.BlockSpec((tm, d), lambda i, l: (i, 0)),
            mod_spec(2),
            pl.BlockSpec((None, 1, d), lambda i, l: (l[0], 0, 0)),
            mod_spec(3), mod_spec(4),
        ],
        out_specs=[pl.BlockSpec((tm, d), lambda i, l: (i, 0)),
                   pl.BlockSpec((tm, d), lambda i, l: (i, 0))],
        scratch_shapes=[pltpu.VMEM((groups, POOL_HALO, gd), BF16)],
    )
    return pl.pallas_call(
        functools.partial(_outproj_kernel, tiles_per_seq=per_seq),
        grid_spec=grid_spec,
        out_shape=[jax.ShapeDtypeStruct((t, d), F32), jax.ShapeDtypeStruct((t, d), BF16)],
        compiler_params=_params(1),
        name="out_proj",
    )(l, ya, proj, pool_w, pool_scale.reshape(depth, groups, 1, gd), w_out, x2d, mods, gains, mods, mods)


def _shift_rows(a, k, halo):
    rolled = pltpu.roll(a, k, axis=0)
    prev = pltpu.roll(halo, k, axis=0)
    ri = lax.broadcasted_iota(jnp.int32, halo.shape, 0)
    top = jnp.where(ri < k, prev, rolled[0:halo.shape[0]])
    return jnp.concatenate([top, rolled[halo.shape[0]:]], axis=0)


def _ffn_kernel(l_ref, x_hbm, h_ref, gate_ref, fg_ref, wa_ref, wv_ref, cw_ref, cb_ref, wd_ref,
                o_hbm, acc_ref, halo_ref, res_ref, sem_ref, *, tiles_per_seq, depth):
    i = pl.program_id(0)
    j = pl.program_id(1)
    n_tiles = pl.num_programs(0)
    n_steps = pl.num_programs(1)
    halo_rows = halo_ref.shape[1]
    tm = acc_ref.shape[0]

    def fetch_rows(tile):
        return pltpu.make_async_copy(x_hbm.at[pl.ds(tile * tm, tm)], res_ref, sem_ref.at[0])

    def write_rows(tile):
        return pltpu.make_async_copy(res_ref, o_hbm.at[pl.ds(tile * tm, tm)], sem_ref.at[1])

    @pl.when(j == n_steps - 2)
    def _():
        @pl.when(i > 0)
        def _():
            write_rows(i - 1).wait()

        fetch_rows(i).start()

    @pl.when(j == 0)
    def _():
        acc_ref[...] = jnp.zeros_like(acc_ref)

    @pl.when(i % tiles_per_seq == 0)
    def _():
        halo_ref[j] = jnp.zeros(halo_ref.shape[1:], F32)

    h = h_ref[...]
    a = jnp.dot(h, wa_ref[...], preferred_element_type=F32)
    v = jnp.dot(h, wv_ref[...], preferred_element_type=F32)
    halo = halo_ref[j]
    cw = cw_ref[...]
    conv = cb_ref[...] + cw[CONV_WIDTH - 1:CONV_WIDTH] * a
    for back in range(1, CONV_WIDTH):
        tap = CONV_WIDTH - 1 - back
        conv = conv + cw[tap:tap + 1] * _shift_rows(a, back, halo)
    halo_ref[j] = a[a.shape[0] - halo_rows:, :]
    glu = (_silu(conv) * v).astype(BF16)
    acc_ref[...] += jnp.dot(glu, wd_ref[...].astype(BF16), preferred_element_type=F32)

    @pl.when(j == n_steps - 1)
    def _():
        fetch_rows(i).wait()
        last_layer = l_ref[0] == depth - 1

        @pl.when(jnp.logical_not(last_layer))
        def _():
            res_ref[...] = res_ref[...] + gate_ref[...] * acc_ref[...]

        @pl.when(last_layer)
        def _():
            xf = res_ref[...] + gate_ref[...] * acc_ref[...]
            ms = jnp.mean(xf * xf, axis=-1, keepdims=True)
            res_ref[...] = xf * lax.rsqrt(ms + EPS) * fg_ref[...]

        write_rows(i).start()

        @pl.when(i == n_tiles - 1)
        def _():
            write_rows(i).wait()


def _ffn(l, x2d, h2d, mods, final_g, w_up, conv_w, conv_b, w_down, seq):
    t, d = x2d.shape
    depth, ff, _ = w_down.shape
    tm, tf = FFN_TM, FFN_TF
    per_seq = seq // tm
    n_ff = ff // tf

    grid_spec = pltpu.PrefetchScalarGridSpec(
        num_scalar_prefetch=1,
        grid=(t // tm, n_ff),
        in_specs=[
            pl.BlockSpec(memory_space=pl.ANY),
            pl.BlockSpec((tm, d), lambda i, j, l: (i, 0)),
            pl.BlockSpec((None, None, None, 1, d), lambda i, j, l: (l[0], i // per_seq, 5, 0, 0)),
            pl.BlockSpec((1, d), lambda i, j, l: (0, 0)),
            pl.BlockSpec((None, d, tf), lambda i, j, l: (l[0], 0, j)),
            pl.BlockSpec((None, d, tf), lambda i, j, l: (l[0], 0, n_ff + j)),
            pl.BlockSpec((None, CONV_WIDTH, tf), lambda i, j, l: (l[0], 0, j)),
            pl.BlockSpec((None, 1, tf), lambda i, j, l: (l[0], 0, j)),
            pl.BlockSpec((None, tf, d), lambda i, j, l: (l[0], j, 0)),
        ],
        out_specs=pl.BlockSpec(memory_space=pl.ANY),
        scratch_shapes=[
            pltpu.VMEM((tm, d), F32),
            pltpu.VMEM((n_ff, 8, tf), F32),
            pltpu.VMEM((tm, d), F32),
            pltpu.SemaphoreType.DMA((2,)),
        ],
    )
    return pl.pallas_call(
        functools.partial(_ffn_kernel, tiles_per_seq=per_seq, depth=depth),
        grid_spec=grid_spec,
        out_shape=jax.ShapeDtypeStruct((t, d), F32),
        compiler_params=_params(2),
        name="conv_glu_ffn",
    )(l, x2d, h2d, mods, final_g, w_up, w_up, conv_w, conv_b.reshape(depth, 1, ff), w_down)


def kernel(x, c, ada_w, ada_b, mix_norm_g, w_in, hgrn_lower_bounds, hgrn_norm_g, pool_w, pool_scale,
           w_out, ffn_norm_g, w_up, conv_w, conv_b, w_down, final_norm_g):
    batch, seq, d = x.shape
    depth = w_in.shape[0]
    d_hgrn = hgrn_lower_bounds.shape[1]
    n_heads = d_hgrn // HEAD_DIM

    mods = _modulation(c, ada_w, ada_b).reshape(depth, batch, N_MOD, 1, d)
    w_in_b, w_out_b = w_in.astype(BF16), w_out.astype(BF16)
    w_up_b, pool_w_b = w_up.astype(BF16), pool_w.astype(BF16)
    lbs = hgrn_lower_bounds.reshape(depth, n_heads, HEAD_DIM).transpose(1, 0, 2)
    mix_g = mix_norm_g.reshape(depth, 1, d)
    ffn_g = ffn_norm_g.reshape(depth, 1, d)
    head_g = hgrn_norm_g.reshape(depth, 1, HEAD_DIM)
    final_g = final_norm_g.reshape(1, d)
    level_mat = jnp.asarray(_hgrn_level_matrix(HGRN_CHUNK), dtype=BF16)

    xc = x.reshape(batch * seq, d)
    for li in range(depth):
        l = jnp.full((1,), li, jnp.int32)
        proj = _in_proj(l, xc, mix_g, mods, w_in_b, seq)
        ya = _hgrn(l, proj, lbs, head_g, level_mat, batch, seq)
        x1, h2 = _out_proj(l, ya, proj, pool_w_b, pool_scale, w_out_b, xc, ffn_g, mods, seq)
        xc = _ffn(l, x1, h2, mods, final_g, w_up_b, conv_w, conv_b, w_down, seq)
    return xc.reshape(batch, seq, d)
```

```python
import functools

import jax
import jax.numpy as jnp
import numpy as np
from jax import lax
from jax.experimental import pallas as pl
from jax.experimental.pallas import tpu as pltpu

F32 = jnp.float32
BF16 = jnp.bfloat16

LANES = 128
HEAD_DIM = 128
POOL_WINDOWS = (2, 4, 8, 16)
POOL_HALO = 128
CONV_WIDTH = 3
N_MOD = 6
EPS = 1e-6
LOG2E = 1.4426950408889634
VMEM_LIMIT = 56 * 1024 * 1024

HGRN_CHUNK = 128
HGRN_TILE = 1024
INPROJ_TM, INPROJ_TN = 512, 1024
OUTPROJ_TM = 512
FFN_TM, FFN_TF = 1024, 512
MOD_TN = 1024


def _params(n_axes, vmem=VMEM_LIMIT):
    return pltpu.CompilerParams(
        dimension_semantics=("arbitrary",) * n_axes, vmem_limit_bytes=vmem)


def _norm_mod(xf, gain, scale, shift):
    ms = jnp.mean(xf * xf, axis=-1, keepdims=True)
    y = xf * lax.rsqrt(ms + EPS) * gain
    return y * (1.0 + scale) + shift


def _silu(x):
    return (0.5 * x) * (1.0 + jnp.tanh(0.5 * x))


def _mod_kernel(c_ref, w_ref, b_ref, o_ref):
    c = c_ref[...]
    c_act = _silu(c).astype(BF16)
    acc = jnp.dot(c_act, w_ref[...].astype(BF16), preferred_element_type=F32)
    o_ref[...] = acc + b_ref[...]


def _modulation(c, ada_w, ada_b):
    depth, d, n = ada_w.shape
    b = c.shape[0]
    return pl.pallas_call(
        _mod_kernel,
        grid=(depth, n // MOD_TN),
        in_specs=[
            pl.BlockSpec((b, d), lambda l, j: (0, 0)),
            pl.BlockSpec((None, d, MOD_TN), lambda l, j: (l, 0, j)),
            pl.BlockSpec((None, 1, MOD_TN), lambda l, j: (l, 0, j)),
        ],
        out_specs=pl.BlockSpec((None, b, MOD_TN), lambda l, j: (l, 0, j)),
        out_shape=jax.ShapeDtypeStruct((depth, b, n), F32),
        compiler_params=_params(2),
        name="ada_mod",
    )(c, ada_w, ada_b.reshape(depth, 1, n))


def _inproj_kernel(l_ref, x_ref, g_ref, sh_ref, sc_ref, w_ref, o_ref):
    h = _norm_mod(x_ref[...], g_ref[...], sc_ref[...], sh_ref[...]).astype(BF16)
    blocks_per_dot = INPROJ_TN // LANES
    for j in range(0, o_ref.shape[0], blocks_per_dot):
        acc = jnp.dot(h, w_ref[:, j * LANES:(j + blocks_per_dot) * LANES], preferred_element_type=F32)
        for blk in range(blocks_per_dot):
            o_ref[j + blk] = acc[:, blk * LANES:(blk + 1) * LANES].astype(BF16)


def _in_proj(l, x2d, gains, mods, w_in, seq):
    t, d = x2d.shape
    n = w_in.shape[2]
    tm = INPROJ_TM
    per_seq = seq // tm
    grid_spec = pltpu.PrefetchScalarGridSpec(
        num_scalar_prefetch=1,
        grid=(t // tm,),
        in_specs=[
            pl.BlockSpec((tm, d), lambda i, l: (i, 0)),
            pl.BlockSpec((None, 1, d), lambda i, l: (l[0], 0, 0)),
            pl.BlockSpec((None, None, None, 1, d), lambda i, l: (l[0], i // per_seq, 0, 0, 0)),
            pl.BlockSpec((None, None, None, 1, d), lambda i, l: (l[0], i // per_seq, 1, 0, 0)),
            pl.BlockSpec((None, d, n), lambda i, l: (l[0], 0, 0), pipeline_mode=pl.Buffered(1)),
        ],
        out_specs=pl.BlockSpec((n // LANES, tm, LANES), lambda i, l: (0, i, 0)),
    )
    return pl.pallas_call(
        _inproj_kernel,
        grid_spec=grid_spec,
        out_shape=jax.ShapeDtypeStruct((n // LANES, t, LANES), BF16),
        compiler_params=_params(1),
        name="in_proj",
    )(l, x2d, gains, mods, mods, w_in)


def _hgrn_level_matrix(chunk):
    t = np.arange(chunk)[:, None]
    r = np.arange(chunk)[None, :]
    low = (r <= t).astype(np.float32)
    mats = [low]
    m = 1
    while m < chunk:
        anchor = (t // (2 * m)) * (2 * m) + m - 1
        sign = np.where(t > anchor, 1.0, -1.0)
        mats.append(sign * (low - (r <= anchor).astype(np.float32)))
        m *= 2
    stacked = np.concatenate(mats, axis=0)
    return np.concatenate([stacked] * 2, axis=1)


_NT = (((1,), (1,)), ((), ()))
_TN = (((0,), (0,)), ((), ()))


def _hgrn_kernel(l_ref, q_ref, f_ref, i_ref, g_ref, lb_ref, ng_ref, lm_ref, o_ref,
                 st_ref, lvl_ref, qk0_ref, qk1_ref, bd0_ref, bd1_ref):
    chunk = HGRN_CHUNK
    n_levels = chunk.bit_length() - 1
    n_heads, tile, _ = q_ref.shape
    n_chunks = tile // chunk
    layer = l_ref[0]

    @pl.when((pl.program_id(0) == 0) & (pl.program_id(1) == 0))
    def _():
        trow = lax.broadcasted_iota(jnp.int32, (chunk, chunk), 0)
        tcol = lax.broadcasted_iota(jnp.int32, (chunk, chunk), 1)
        txor = trow ^ tcol
        top = jnp.full((chunk, chunk), -1, jnp.int32)
        for k in range(n_levels):
            top = top + ((txor >> k) != 0).astype(jnp.int32)
        lvl_ref[...] = jnp.where(trow > tcol, top, jnp.where(trow == tcol, n_levels, -1))

    @pl.when(pl.program_id(1) == 0)
    def _():
        st_ref[...] = jnp.zeros_like(st_ref)

    gain = ng_ref[...]

    def gates(h, qk_ref, bd_ref):
        lbraw = lb_ref[h]
        ex = jnp.exp(lbraw - jnp.max(lbraw, axis=0, keepdims=True))
        p = ex / jnp.sum(ex, axis=0, keepdims=True)
        lrow = lax.broadcasted_iota(jnp.int32, p.shape, 0)
        cum = jnp.sum(jnp.where(lrow <= layer, p, 0.0), axis=0, keepdims=True)
        lb = jnp.clip(cum - p[0:1], 0.0, 1.0)
        half_key = 0.5 * (1.0 - lb)

        parts = []
        for c in range(n_chunks):
            rows = pl.ds(c * chunk, chunk)
            z = f_ref[h, rows, :].astype(F32)
            e = jnp.exp2(z * (-LOG2E))
            logf = (jnp.log(1.0 + lb * e) - jnp.log(1.0 + e)) * LOG2E
            key = half_key - half_key * jnp.tanh(0.5 * z)
            qraw = q_ref[h, rows, :].astype(F32)
            qq = (qraw * (0.5 * HEAD_DIM ** -0.5)) * (1.0 + jnp.tanh(0.5 * qraw))
            qk_ref[0, rows, :] = qq.astype(BF16)
            qk_ref[1, rows, :] = key.astype(BF16)
            hi = logf.astype(BF16)
            lo = (logf - hi.astype(F32)).astype(BF16)
            parts.append(jnp.concatenate([hi, lo], axis=0))

        bd = jnp.dot(lm_ref[...], jnp.concatenate(parts, axis=1), preferred_element_type=F32)
        for c in range(n_chunks):
            bd_ref[c] = bd[:, c * HEAD_DIM:(c + 1) * HEAD_DIM]

    def mix(h, qk_ref, bd_ref):
        st = st_ref[h]
        lvl = lvl_ref[...]
        all_scores = []
        for c in range(n_chunks):
            rows = pl.ds(c * chunk, chunk)
            s0 = lax.dot_general(qk_ref[0, rows, :], qk_ref[1, rows, :], _NT, preferred_element_type=F32)
            all_scores.append(jnp.where(lvl == n_levels, s0, 0.0))
        for k in range(n_levels):
            for c in range(n_chunks):
                rows = pl.ds(c * chunk, chunk)
                w = jnp.exp2(bd_ref[c, pl.ds(chunk * (k + 1), chunk), :]).astype(BF16)
                sk = lax.dot_general(qk_ref[0, rows, :] * w, qk_ref[1, rows, :] * w, _NT,
                                     preferred_element_type=F32)
                all_scores[c] = jnp.where(lvl == k, sk, all_scores[c])
        all_scores = [s.astype(BF16) for s in all_scores]

        for c in range(n_chunks):
            rows = pl.ds(c * chunk, chunk)
            qq = qk_ref[0, rows, :]
            key = qk_ref[1, rows, :]
            val = i_ref[h, rows, :]
            b = bd_ref[c, pl.ds(0, chunk), :]
            o = jnp.dot(all_scores[c], val, preferred_element_type=F32)
            o = o + lax.dot_general(qq * jnp.exp2(b).astype(BF16), st.astype(BF16), _NT,
                                    preferred_element_type=F32)

            b_last = b[chunk - 1:chunk]
            k_out = key * jnp.exp2(b_last - b).astype(BF16)
            st = st * jnp.exp2(b_last) + lax.dot_general(val, k_out, _TN, preferred_element_type=F32)

            ms = jnp.mean(o * o, axis=-1, keepdims=True)
            on = o * lax.rsqrt(ms + EPS) * gain
            graw = g_ref[h, rows, :].astype(F32)
            o_ref[h, rows, :] = (on * _silu(graw)).astype(BF16)
        st_ref[h] = st

    gates(0, qk0_ref, bd0_ref)

    def pair_body(j, carry):
        h0 = 2 * j
        mix(h0, qk0_ref, bd0_ref)
        gates(h0 + 1, qk1_ref, bd1_ref)
        mix(h0 + 1, qk1_ref, bd1_ref)
        gates(h0 + 2, qk0_ref, bd0_ref)
        return carry

    lax.fori_loop(0, n_heads // 2 - 1, pair_body, 0)
    mix(n_heads - 2, qk0_ref, bd0_ref)
    gates(n_heads - 1, qk1_ref, bd1_ref)
    mix(n_heads - 1, qk1_ref, bd1_ref)


def _hgrn(l, proj, lbs, norm_g, level_mat, batch, seq):
    n_heads = lbs.shape[0]
    t = proj.shape[1]
    tile = HGRN_TILE
    per_seq = seq // tile
    n_chunks = tile // HGRN_CHUNK
    proj5 = proj.reshape(5, n_heads, t, LANES)

    def sec_spec(sec):
        return pl.BlockSpec((None, n_heads, tile, LANES),
                            lambda b, s, l: (sec, 0, b * per_seq + s, 0))

    grid_spec = pltpu.PrefetchScalarGridSpec(
        num_scalar_prefetch=1,
        grid=(batch, per_seq),
        in_specs=[
            sec_spec(0), sec_spec(1), sec_spec(2), sec_spec(3),
            pl.BlockSpec(lbs.shape, lambda b, s, l: (0, 0, 0)),
            pl.BlockSpec((None, 1, HEAD_DIM), lambda b, s, l: (l[0], 0, 0)),
            pl.BlockSpec(level_mat.shape, lambda b, s, l: (0, 0)),
        ],
        out_specs=pl.BlockSpec((n_heads, tile, LANES), lambda b, s, l: (0, b * per_seq + s, 0)),
        scratch_shapes=[pltpu.VMEM((n_heads, HEAD_DIM, HEAD_DIM), F32),
                        pltpu.VMEM((HGRN_CHUNK, HGRN_CHUNK), jnp.int32),
                        pltpu.VMEM((2, tile, HEAD_DIM), BF16),
                        pltpu.VMEM((2, tile, HEAD_DIM), BF16),
                        pltpu.VMEM((n_chunks, level_mat.shape[0], HEAD_DIM), F32),
                        pltpu.VMEM((n_chunks, level_mat.shape[0], HEAD_DIM), F32)],
    )
    return pl.pallas_call(
        _hgrn_kernel,
        grid_spec=grid_spec,
        out_shape=jax.ShapeDtypeStruct((n_heads, t, LANES), BF16),
        compiler_params=_params(2),
        name="hgrn2",
    )(l, proj5, proj5, proj5, proj5, lbs, norm_g, level_mat)


def _pool_group(u, halo, width, pos, pw, ps):
    tile = u.shape[0]
    ext = jnp.concatenate([halo, u], axis=0)
    trow = lax.broadcasted_iota(jnp.int32, (POOL_HALO, 2 * POOL_HALO), 0) + POOL_HALO
    rcol = lax.broadcasted_iota(jnp.int32, (POOL_HALO, 2 * POOL_HALO), 1)
    band = ((rcol <= trow) & (rcol > trow - width)).astype(BF16)
    sums = jnp.concatenate(
        [jnp.dot(band, ext[r * POOL_HALO:(r + 2) * POOL_HALO], preferred_element_type=F32)
         for r in range(tile // POOL_HALO)], axis=0)
    count = jnp.minimum(pos, width).astype(F32)
    pooled = sums / count - u.astype(F32)
    y = jnp.dot(pooled.astype(BF16), pw, preferred_element_type=F32)
    return (y * ps).astype(BF16)


def _outproj_kernel(l_ref, ya_ref, u_ref, pw_ref, ps_ref, w_ref, x_ref, gate_ref, g_ref, sh_ref, sc_ref,
                    o_ref, h_ref, halo_ref, *, tiles_per_seq):
    tile = x_ref.shape[0]
    n_groups = pw_ref.shape[0]
    blocks_per_group = u_ref.shape[0] // n_groups
    tile_in_seq = pl.program_id(0) % tiles_per_seq

    @pl.when(tile_in_seq == 0)
    def _():
        halo_ref[...] = jnp.zeros_like(halo_ref)

    pos = lax.broadcasted_iota(jnp.int32, (tile, 1), 0) + tile_in_seq * tile + 1
    y_parts = [ya_ref[h] for h in range(ya_ref.shape[0])]
    for grp in range(n_groups):
        u = jnp.concatenate([u_ref[grp * blocks_per_group + k] for k in range(blocks_per_group)], axis=1)
        y_parts.append(_pool_group(u, halo_ref[grp], POOL_WINDOWS[grp], pos, pw_ref[grp], ps_ref[grp]))
        halo_ref[grp] = u[tile - POOL_HALO:, :]

    acc = jnp.dot(jnp.concatenate(y_parts, axis=1), w_ref[...].astype(BF16), preferred_element_type=F32)
    x1 = x_ref[...] + gate_ref[...] * acc
    o_ref[...] = x1
    h_ref[...] = _norm_mod(x1, g_ref[...], sc_ref[...], sh_ref[...]).astype(BF16)


def _out_proj(l, ya, proj, pool_w, pool_scale, w_out, x2d, gains, mods, seq):
    t, d = x2d.shape
    n_heads = ya.shape[0]
    depth, groups, gd, _ = pool_w.shape
    pool_blocks = groups * gd // LANES
    tm = OUTPROJ_TM
    per_seq = seq // tm

    def mod_spec(k):
        return pl.BlockSpec((None, None, None, 1, d), lambda i, l: (l[0], i // per_seq, k, 0, 0))

    grid_spec = pltpu.PrefetchScalarGridSpec(
        num_scalar_prefetch=1,
        grid=(t // tm,),
        in_specs=[
            pl.BlockSpec((n_heads, tm, LANES), lambda i, l: (0, i, 0)),
            pl.BlockSpec((pool_blocks, tm, LANES), lambda i, l: (proj.shape[0] // pool_blocks - 1, i, 0)),
            pl.BlockSpec((None, groups, gd, gd), lambda i, l: (l[0], 0, 0, 0)),
            pl.BlockSpec((None, groups, 1, gd), lambda i, l: (l[0], 0, 0, 0)),
            pl.BlockSpec((None, d, d), lambda i, l: (l[0], 0, 0), pipeline_mode=pl.Buffered(1)),
            pl.BlockSpec((tm, d), lambda i, l: (i, 0)),
            mod_spec(2),
            pl.BlockSpec((None, 1, d), lambda i, l: (l[0], 0, 0)),
            mod_spec(3), mod_spec(4),
        ],
        out_specs=[pl.BlockSpec((tm, d), lambda i, l: (i, 0)),
                   pl.BlockSpec((tm, d), lambda i, l: (i, 0))],
        scratch_shapes=[pltpu.VMEM((groups, POOL_HALO, gd), BF16)],
    )
    return pl.pallas_call(
        functools.partial(_outproj_kernel, tiles_per_seq=per_seq),
        grid_spec=grid_spec,
        out_shape=[jax.ShapeDtypeStruct((t, d), F32), jax.ShapeDtypeStruct((t, d), BF16)],
        compiler_params=_params(1),
        name="out_proj",
    )(l, ya, proj, pool_w, pool_scale.reshape(depth, groups, 1, gd), w_out, x2d, mods, gains, mods, mods)


def _shift_rows(a, k, halo):
    rolled = pltpu.roll(a, k, axis=0)
    prev = pltpu.roll(halo, k, axis=0)
    ri = lax.broadcasted_iota(jnp.int32, halo.shape, 0)
    top = jnp.where(ri < k, prev, rolled[0:halo.shape[0]])
    return jnp.concatenate([top, rolled[halo.shape[0]:]], axis=0)


def _ffn_kernel(l_ref, x_hbm, h_ref, gate_ref, fg_ref, wa_ref, wv_ref, cw_ref, cb_ref, wd_ref,
                o_hbm, acc_ref, halo_ref, res_ref, sem_ref, *, tiles_per_seq, depth):
    i = pl.program_id(0)
    j = pl.program_id(1)
    n_tiles = pl.num_programs(0)
    n_steps = pl.num_programs(1)
    halo_rows = halo_ref.shape[1]
    tm = acc_ref.shape[0]

    def fetch_rows(tile):
        return pltpu.make_async_copy(x_hbm.at[pl.ds(tile * tm, tm)], res_ref, sem_ref.at[0])

    def write_rows(tile):
        return pltpu.make_async_copy(res_ref, o_hbm.at[pl.ds(tile * tm, tm)], sem_ref.at[1])

    @pl.when(j == n_steps - 2)
    def _():
        @pl.when(i > 0)
        def _():
            write_rows(i - 1).wait()

        fetch_rows(i).start()

    @pl.when(j == 0)
    def _():
        acc_ref[...] = jnp.zeros_like(acc_ref)

    @pl.when(i % tiles_per_seq == 0)
    def _():
        halo_ref[j] = jnp.zeros(halo_ref.shape[1:], F32)

    h = h_ref[...]
    a = jnp.dot(h, wa_ref[...], preferred_element_type=F32)
    v = jnp.dot(h, wv_ref[...], preferred_element_type=F32)
    halo = halo_ref[j]
    cw = cw_ref[...]
    conv = cb_ref[...] + cw[CONV_WIDTH - 1:CONV_WIDTH] * a
    for back in range(1, CONV_WIDTH):
        tap = CONV_WIDTH - 1 - back
        conv = conv + cw[tap:tap + 1] * _shift_rows(a, back, halo)
    halo_ref[j] = a[a.shape[0] - halo_rows:, :]
    glu = (_silu(conv) * v).astype(BF16)
    acc_ref[...] += jnp.dot(glu, wd_ref[...].astype(BF16), preferred_element_type=F32)

    @pl.when(j == n_steps - 1)
    def _():
        fetch_rows(i).wait()
        last_layer = l_ref[0] == depth - 1

        @pl.when(jnp.logical_not(last_layer))
        def _():
            res_ref[...] = res_ref[...] + gate_ref[...] * acc_ref[...]

        @pl.when(last_layer)
        def _():
            xf = res_ref[...] + gate_ref[...] * acc_ref[...]
            ms = jnp.mean(xf * xf, axis=-1, keepdims=True)
            res_ref[...] = xf * lax.rsqrt(ms + EPS) * fg_ref[...]

        write_rows(i).start()

        @pl.when(i == n_tiles - 1)
        def _():
            write_rows(i).wait()


def _ffn(l, x2d, h2d, mods, final_g, w_up, conv_w, conv_b, w_down, seq):
    t, d = x2d.shape
    depth, ff, _ = w_down.shape
    tm, tf = FFN_TM, FFN_TF
    per_seq = seq // tm
    n_ff = ff // tf

    grid_spec = pltpu.PrefetchScalarGridSpec(
        num_scalar_prefetch=1,
        grid=(t // tm, n_ff),
        in_specs=[
            pl.BlockSpec(memory_space=pl.ANY),
            pl.BlockSpec((tm, d), lambda i, j, l: (i, 0)),
            pl.BlockSpec((None, None, None, 1, d), lambda i, j, l: (l[0], i // per_seq, 5, 0, 0)),
            pl.BlockSpec((1, d), lambda i, j, l: (0, 0)),
            pl.BlockSpec((None, d, tf), lambda i, j, l: (l[0], 0, j)),
            pl.BlockSpec((None, d, tf), lambda i, j, l: (l[0], 0, n_ff + j)),
            pl.BlockSpec((None, CONV_WIDTH, tf), lambda i, j, l: (l[0], 0, j)),
            pl.BlockSpec((None, 1, tf), lambda i, j, l: (l[0], 0, j)),
            pl.BlockSpec((None, tf, d), lambda i, j, l: (l[0], j, 0)),
        ],
        out_specs=pl.BlockSpec(memory_space=pl.ANY),
        scratch_shapes=[
            pltpu.VMEM((tm, d), F32),
            pltpu.VMEM((n_ff, 8, tf), F32),
            pltpu.VMEM((tm, d), F32),
            pltpu.SemaphoreType.DMA((2,)),
        ],
    )
    return pl.pallas_call(
        functools.partial(_ffn_kernel, tiles_per_seq=per_seq, depth=depth),
        grid_spec=grid_spec,
        out_shape=jax.ShapeDtypeStruct((t, d), F32),
        compiler_params=_params(2),
        name="conv_glu_ffn",
    )(l, x2d, h2d, mods, final_g, w_up, w_up, conv_w, conv_b.reshape(depth, 1, ff), w_down)


def kernel(x, c, ada_w, ada_b, mix_norm_g, w_in, hgrn_lower_bounds, hgrn_norm_g, pool_w, pool_scale,
           w_out, ffn_norm_g, w_up, conv_w, conv_b, w_down, final_norm_g):
    batch, seq, d = x.shape
    depth = w_in.shape[0]
    d_hgrn = hgrn_lower_bounds.shape[1]
    n_heads = d_hgrn // HEAD_DIM

    mods = _modulation(c, ada_w, ada_b).reshape(depth, batch, N_MOD, 1, d)
    w_in_b = w_in.astype(BF16)
    w_up_b, pool_w_b = w_up.astype(BF16), pool_w.astype(BF16)
    lbs = hgrn_lower_bounds.reshape(depth, n_heads, HEAD_DIM).transpose(1, 0, 2)
    mix_g = mix_norm_g.reshape(depth, 1, d)
    ffn_g = ffn_norm_g.reshape(depth, 1, d)
    head_g = hgrn_norm_g.reshape(depth, 1, HEAD_DIM)
    final_g = final_norm_g.reshape(1, d)
    level_mat = jnp.asarray(_hgrn_level_matrix(HGRN_CHUNK), dtype=BF16)

    xc = x.reshape(batch * seq, d)
    for li in range(depth):
        l = jnp.full((1,), li, jnp.int32)
        proj = _in_proj(l, xc, mix_g, mods, w_in_b, seq)
        ya = _hgrn(l, proj, lbs, head_g, level_mat, batch, seq)
        x1, h2 = _out_proj(l, ya, proj, pool_w_b, pool_scale, w_out, xc, ffn_g, mods, seq)
        xc = _ffn(l, x1, h2, mods, final_g, w_up_b, conv_w, conv_b, w_down, seq)
    return xc.reshape(batch, seq, d)
```
